```python
import jax, jax.numpy as jnp
from jax import lax
import numpy as np

D_MODEL = 4096
BATCH = 4
SEQ = 4096
DEPTH = 1

CHUNK = 64
N_LEFT_CHUNKS = 8
BAND = (N_LEFT_CHUNKS + 1) * CHUNK
ATTN_WIDTH = D_MODEL // 2
ATTN_HEAD_DIM = 128
ATTN_HEADS = ATTN_WIDTH // ATTN_HEAD_DIM
MAX_REL = 128
N_REL = 2 * MAX_REL + 1
POOL_WIDTH = D_MODEL // 2
POOL_WINDOWS = (2, 4, 8, 16)
N_POOL_GROUPS = len(POOL_WINDOWS)
POOL_GROUP_DIM = POOL_WIDTH // N_POOL_GROUPS
N_BRANCHES = 2
IN_COLS = 4 * ATTN_WIDTH + 2 * POOL_WIDTH + N_BRANCHES * D_MODEL
EPS = 1e-6

kernel_name = "hybrid_chunk_attn_pool_gated_block"


def _rmsnorm(x, g):
    xf = x.astype(jnp.float32)
    y = xf * lax.rsqrt(jnp.mean(xf * xf, axis=-1, keepdims=True) + EPS)
    return (y * g.astype(jnp.float32)).astype(x.dtype)


def _chunk_band_attention(q, k, v, rel_bias):
    B, S, H, Dh = q.shape
    nc = S // CHUNK
    pad = N_LEFT_CHUNKS * CHUNK
    kp = jnp.pad(k, ((0, 0), (pad, 0), (0, 0), (0, 0)))
    vp = jnp.pad(v, ((0, 0), (pad, 0), (0, 0), (0, 0)))
    qc = q.reshape(B, nc, CHUNK, H, Dh).transpose(1, 0, 2, 3, 4)
    rel = pad + jnp.arange(CHUNK)[:, None] - jnp.arange(BAND)[None, :]
    bias = rel_bias[:, jnp.clip(rel, -MAX_REL, MAX_REL) + MAX_REL].astype(jnp.float32)
    scale = Dh ** -0.5
    band_offsets = jnp.arange(BAND) - pad

    def one_chunk(args):
        c, qb = args
        start = c * CHUNK
        kb = lax.dynamic_slice_in_dim(kp, start, BAND, axis=1)
        vb = lax.dynamic_slice_in_dim(vp, start, BAND, axis=1)
        s = jnp.einsum('bqhd,bkhd->bhqk', qb, kb,
                       preferred_element_type=jnp.float32) * scale + bias
        valid = (start + band_offsets) >= 0
        s = jnp.where(valid[None, None, None, :], s, -jnp.inf)
        p = jax.nn.softmax(s, axis=-1)
        return jnp.einsum('bhqk,bkhd->bqhd', p.astype(vb.dtype), vb)

    out = lax.map(one_chunk, (jnp.arange(nc), qc))
    return out.transpose(1, 0, 2, 3, 4).reshape(B, S, H * Dh)


def _multiscale_pool(u, pool_w, pool_scale):
    B, S, W = u.shape
    uf = u.astype(jnp.float32).reshape(B, S, N_POOL_GROUPS, POOL_GROUP_DIM)
    cs = jnp.pad(jnp.cumsum(uf, axis=1), ((0, 0), (1, 0), (0, 0), (0, 0)))
    t = jnp.arange(S)
    means = []
    for g, w in enumerate(POOL_WINDOWS):
        csg = cs[:, :, g]
        lo = jnp.pad(csg[:, :S + 1 - w], ((0, 0), (w - 1, 0), (0, 0)))
        cnt = jnp.minimum(t + 1, w).astype(jnp.float32)[None, :, None]
        means.append((csg[:, 1:] - lo) / cnt)
    mean = jnp.stack(means, axis=2)
    d = (mean - uf).astype(u.dtype)
    y = jnp.einsum('bsgc,gcd->bsgd', d, pool_w)
    return y.reshape(B, S, W) * pool_scale


def setup_inputs(seed: int = 0) -> dict:
    key = jax.random.key(seed)
    ks = jax.random.split(key, 12)
    f = jnp.float32
    x = jax.random.normal(ks[0], (BATCH, SEQ, D_MODEL), f)
    norm_gain = 1.0 + 0.1 * jax.random.normal(ks[1], (D_MODEL,), f)
    w_in = jax.random.normal(ks[2], (D_MODEL, IN_COLS), f) * D_MODEL ** -0.5
    rel_bias = 0.5 * jax.random.normal(ks[3], (ATTN_HEADS, N_REL), f)
    pool_w = jax.random.normal(ks[4], (N_POOL_GROUPS, POOL_GROUP_DIM, POOL_GROUP_DIM), f) * POOL_GROUP_DIM ** -0.5
    pool_scale = 1.0 + 0.1 * jax.random.normal(ks[5], (POOL_WIDTH,), f)
    w_out_attn = jax.random.normal(ks[6], (ATTN_WIDTH, D_MODEL), f) * ATTN_WIDTH ** -0.5
    w_out_pool = jax.random.normal(ks[7], (POOL_WIDTH, D_MODEL), f) * POOL_WIDTH ** -0.5
    gate_bias = 0.1 * jax.random.normal(ks[8], (N_BRANCHES, D_MODEL), f)
    w_out = jax.random.normal(ks[9], (D_MODEL, D_MODEL), f) * D_MODEL ** -0.5
    final_gain = 1.0 + 0.1 * jax.random.normal(ks[10], (D_MODEL,), f)
    return {"x": x, "norm_gain": norm_gain, "w_in": w_in, "rel_bias": rel_bias,
            "pool_w": pool_w, "pool_scale": pool_scale, "w_out_attn": w_out_attn,
            "w_out_pool": w_out_pool, "gate_bias": gate_bias, "w_out": w_out,
            "final_gain": final_gain}


def reference(x, norm_gain, w_in, rel_bias, pool_w, pool_scale, w_out_attn,
              w_out_pool, gate_bias, w_out, final_gain):
    B, S, D = x.shape
    A, P = ATTN_WIDTH, POOL_WIDTH
    for _ in range(DEPTH):
        h = _rmsnorm(x, norm_gain)
        proj = jnp.einsum('bsd,dn->bsn', h, w_in)
        o = 0
        q = proj[..., o:o + A]; o += A
        k = proj[..., o:o + A]; o += A
        v = proj[..., o:o + A]; o += A
        z_attn = proj[..., o:o + A]; o += A
        u_pool = proj[..., o:o + P]; o += P
        z_pool = proj[..., o:o + P]; o += P
        g_attn = proj[..., o:o + D]; o += D
        g_pool = proj[..., o:o + D]

        hs = (B, S, ATTN_HEADS, ATTN_HEAD_DIM)
        y_attn = _chunk_band_attention(q.reshape(hs), k.reshape(hs), v.reshape(hs), rel_bias)
        y_attn = y_attn * jax.nn.silu(z_attn)
        y_pool = _multiscale_pool(u_pool, pool_w, pool_scale) * jax.nn.silu(z_pool)

        m = (jax.nn.sigmoid(g_attn + gate_bias[0]) * jnp.einsum('bsa,ad->bsd', y_attn, w_out_attn)
             + jax.nn.sigmoid(g_pool + gate_bias[1]) * jnp.einsum('bsp,pd->bsd', y_pool, w_out_pool))
        x = x + jnp.einsum('bsd,de->bse', m, w_out)
    return _rmsnorm(x, final_gain)
```

```python
import functools

import jax
import jax.numpy as jnp
from jax import lax
from jax.experimental import pallas as pl
from jax.experimental.pallas import tpu as pltpu

D_MODEL = 4096
CHUNK = 64
N_LEFT_CHUNKS = 8
ATTN_WIDTH = D_MODEL // 2
HEAD_DIM = 128
N_HEADS = ATTN_WIDTH // HEAD_DIM
MAX_REL = 128
POOL_WIDTH = D_MODEL // 2
POOL_WINDOWS = (2, 4, 8, 16)
POOL_GROUP_DIM = POOL_WIDTH // len(POOL_WINDOWS)
EPS = 1e-6

Q_OFF = 0
K_OFF = ATTN_WIDTH
V_OFF = 2 * ATTN_WIDTH
ZA_OFF = 3 * ATTN_WIDTH
U_OFF = 4 * ATTN_WIDTH
ZP_OFF = U_OFF + POOL_WIDTH
GA_OFF = ZP_OFF + POOL_WIDTH
GP_OFF = GA_OFF + D_MODEL
IN_COLS = GP_OFF + D_MODEL

Q_BLK = 2 * CHUNK
K_WIN = Q_BLK + N_LEFT_CHUNKS * CHUNK
K_BACK = N_LEFT_CHUNKS * CHUNK
HEADS_PER_STEP = 2
NEG = -1e30

POOL_TILE = 512
POOL_SUB = 128

VMEM_LIMIT = 56 * 1024 * 1024


def _params(sem):
    return pltpu.CompilerParams(dimension_semantics=sem, vmem_limit_bytes=VMEM_LIMIT)


def _norm_kernel(x_ref, g_ref, o_ref):
    x = x_ref[...]
    ms = jnp.mean(x * x, axis=-1, keepdims=True)
    o_ref[...] = (x * lax.rsqrt(ms + EPS) * g_ref[...]).astype(o_ref.dtype)


def _rmsnorm_bf16(x2, gain):
    m, d = x2.shape
    tm = 256
    return pl.pallas_call(
        _norm_kernel,
        grid=(m // tm,),
        in_specs=[pl.BlockSpec((tm, d), lambda i: (i, 0)),
                  pl.BlockSpec((1, d), lambda i: (0, 0))],
        out_specs=pl.BlockSpec((tm, d), lambda i: (i, 0)),
        out_shape=jax.ShapeDtypeStruct((m, d), jnp.bfloat16),
        compiler_params=_params(("parallel",)),
        name="rmsnorm_in",
    )(x2, gain.reshape(1, d))


def _mm_kernel(a_ref, b_ref, o_ref):
    o_ref[...] = jnp.dot(a_ref[...], b_ref[...],
                         preferred_element_type=jnp.float32).astype(o_ref.dtype)


def _matmul_bf16(a, b, bm=1024, bn=1024):
    m, k = a.shape
    _, n = b.shape
    return pl.pallas_call(
        _mm_kernel,
        grid=(m // bm, n // bn),
        in_specs=[pl.BlockSpec((bm, k), lambda i, j: (i, 0)),
                  pl.BlockSpec((k, bn), lambda i, j: (0, j))],
        out_specs=pl.BlockSpec((bm, bn), lambda i, j: (i, j)),
        out_shape=jax.ShapeDtypeStruct((m, n), jnp.bfloat16),
        compiler_params=_params(("parallel", "arbitrary")),
        name="in_proj",
    )(a, b)


def _silu(z):
    return z * jax.nn.sigmoid(z)


def _attn_kernel(q_ref, k_ref, v_ref, z_ref, bias_ref, o_ref, *, seq):
    scale = HEAD_DIM ** -0.5
    n_blk = seq // Q_BLK
    n_peel = K_BACK // Q_BLK

    for hh in range(HEADS_PER_STEP):
        cs = slice(hh * HEAD_DIM, (hh + 1) * HEAD_DIM)

        def block(q0, k0, nk, bias):
            q = q_ref[pl.ds(q0, Q_BLK), cs]
            k = k_ref[pl.ds(k0, nk), cs]
            v = v_ref[pl.ds(k0, nk), cs]
            s = lax.dot_general(q, k, (((1,), (1,)), ((), ())),
                                preferred_element_type=jnp.float32)
            s = s * scale + bias
            m = jnp.max(s, axis=-1, keepdims=True)
            p = jnp.exp(s - m)
            l = jnp.sum(p, axis=-1, keepdims=True)
            o = jnp.dot(p.astype(jnp.bfloat16), v, preferred_element_type=jnp.float32)
            o = o / l
            z = z_ref[pl.ds(q0, Q_BLK), cs].astype(jnp.float32)
            o_ref[pl.ds(q0, Q_BLK), cs] = (o * _silu(z)).astype(o_ref.dtype)

        for i in range(n_peel):
            nk = Q_BLK * (i + 1)
            block(i * Q_BLK, 0, nk, bias_ref[hh, :, K_WIN - nk:])

        def body(i, carry):
            q0 = pl.multiple_of(i * Q_BLK, Q_BLK)
            k0 = pl.multiple_of(i * Q_BLK - K_BACK, Q_BLK)
            block(q0, k0, K_WIN, bias_ref[hh])
            return carry

        lax.fori_loop(n_peel, n_blk, body, 0)


def _band_bias(rel_bias):
    r = jnp.arange(Q_BLK)[:, None]
    c = jnp.arange(K_WIN)[None, :]
    rel = r + K_BACK - c
    idx = jnp.clip(rel, -MAX_REL, MAX_REL) + MAX_REL
    qc = r // CHUNK
    kc = c // CHUNK
    valid = (kc >= qc) & (kc <= qc + N_LEFT_CHUNKS)
    b = rel_bias[:, idx].astype(jnp.float32)
    return jnp.where(valid[None], b, NEG)


def _attention(proj, bias, batch, seq):
    hw = HEADS_PER_STEP * HEAD_DIM
    n_hp = N_HEADS // HEADS_PER_STEP

    def col_spec(off):
        base = off // hw
        return pl.BlockSpec((seq, hw), lambda b, h: (b, base + h))

    return pl.pallas_call(
        functools.partial(_attn_kernel, seq=seq),
        grid=(batch, n_hp),
        in_specs=[col_spec(Q_OFF), col_spec(K_OFF), col_spec(V_OFF), col_spec(ZA_OFF),
                  pl.BlockSpec((HEADS_PER_STEP, Q_BLK, K_WIN), lambda b, h: (h, 0, 0))],
        out_specs=pl.BlockSpec((seq, hw), lambda b, h: (b, h)),
        out_shape=jax.ShapeDtypeStruct((batch * seq, ATTN_WIDTH), jnp.bfloat16),
        compiler_params=_params(("parallel", "arbitrary")),
        name="band_attn",
    )(proj, proj, proj, proj, bias)


def _pool_kernel(u_ref, halo_ref, z_ref, a_ref, pw_ref, ps_ref, o_ref, buf_ref):
    s = pl.program_id(1)
    halo = halo_ref[...]
    halo = jnp.where(s == 0, jnp.zeros_like(halo), halo)
    buf_ref[0:POOL_SUB, :] = halo
    buf_ref[POOL_SUB:, :] = u_ref[...]
    t0 = s * POOL_TILE
    row = lax.broadcasted_iota(jnp.int32, (POOL_SUB, 1), 0)
    n_sub = POOL_TILE // POOL_SUB

    for g, w in enumerate(POOL_WINDOWS):
        cs = slice(g * POOL_GROUP_DIM, (g + 1) * POOL_GROUP_DIM)
        ds = []
        for j in range(n_sub):
            win = buf_ref[j * POOL_SUB:(j + 2) * POOL_SUB, cs]
            wsum = jnp.dot(a_ref[g], win, preferred_element_type=jnp.float32)
            t = t0 + j * POOL_SUB + row
            cnt = jnp.minimum(t + 1, w).astype(jnp.float32)
            cur = buf_ref[(j + 1) * POOL_SUB:(j + 2) * POOL_SUB, cs].astype(jnp.float32)
            ds.append((wsum / cnt - cur).astype(jnp.bfloat16))
        d = jnp.concatenate(ds, axis=0)
        y = jnp.dot(d, pw_ref[g], preferred_element_type=jnp.float32)
        z = z_ref[:, cs].astype(jnp.float32)
        o_ref[:, cs] = (y * ps_ref[:, cs] * _silu(z)).astype(o_ref.dtype)


def _window_matrices():
    r = jnp.arange(POOL_SUB)[:, None] + POOL_SUB
    c = jnp.arange(2 * POOL_SUB)[None, :]
    mats = [((c <= r) & (c > r - w)) for w in POOL_WINDOWS]
    return jnp.stack(mats).astype(jnp.bfloat16)


def _pooling(proj, pool_w, pool_scale, batch, seq):
    n_tiles = seq // POOL_TILE
    sub_per_tile = POOL_TILE // POOL_SUB
    sub_per_seq = seq // POOL_SUB
    u_col = U_OFF // POOL_WIDTH
    z_col = ZP_OFF // POOL_WIDTH
    g = len(POOL_WINDOWS)

    def halo_map(b, s):
        return (jnp.maximum(b * sub_per_seq + s * sub_per_tile - 1, 0), u_col)

    return pl.pallas_call(
        _pool_kernel,
        grid=(batch, n_tiles),
        in_specs=[pl.BlockSpec((POOL_TILE, POOL_WIDTH), lambda b, s: (b * n_tiles + s, u_col)),
                  pl.BlockSpec((POOL_SUB, POOL_WIDTH), halo_map),
                  pl.BlockSpec((POOL_TILE, POOL_WIDTH), lambda b, s: (b * n_tiles + s, z_col)),
                  pl.BlockSpec((g, POOL_SUB, 2 * POOL_SUB), lambda b, s: (0, 0, 0)),
                  pl.BlockSpec((g, POOL_GROUP_DIM, POOL_GROUP_DIM), lambda b, s: (0, 0, 0)),
                  pl.BlockSpec((1, POOL_WIDTH), lambda b, s: (0, 0))],
        out_specs=pl.BlockSpec((POOL_TILE, POOL_WIDTH), lambda b, s: (b * n_tiles + s, 0)),
        out_shape=jax.ShapeDtypeStruct((batch * seq, POOL_WIDTH), jnp.bfloat16),
        scratch_shapes=[pltpu.VMEM((POOL_TILE + POOL_SUB, POOL_WIDTH), jnp.bfloat16)],
        compiler_params=_params(("parallel", "arbitrary")),
        name="ms_pool",
    )(proj, proj, proj, _window_matrices(), pool_w.astype(jnp.bfloat16),
      pool_scale.reshape(1, POOL_WIDTH))


def _merge_kernel(ya_ref, yp_ref, wa_ref, wp_ref, ga_ref, gp_ref, gb_ref, o_ref):
    a = jnp.dot(ya_ref[...], wa_ref[...], preferred_element_type=jnp.float32)
    p = jnp.dot(yp_ref[...], wp_ref[...], preferred_element_type=jnp.float32)
    ga = jax.nn.sigmoid(ga_ref[...].astype(jnp.float32) + gb_ref[0:1, :])
    gp = jax.nn.sigmoid(gp_ref[...].astype(jnp.float32) + gb_ref[1:2, :])
    o_ref[...] = (ga * a + gp * p).astype(o_ref.dtype)


def _merge(y_attn, y_pool, wa, wp, proj, gate_bias, bm=1024, bn=1024):
    m = y_attn.shape[0]
    ga_col = GA_OFF // bn
    gp_col = GP_OFF // bn
    return pl.pallas_call(
        _merge_kernel,
        grid=(m // bm, D_MODEL // bn),
        in_specs=[pl.BlockSpec((bm, ATTN_WIDTH), lambda i, j: (i, 0)),
                  pl.BlockSpec((bm, POOL_WIDTH), lambda i, j: (i, 0)),
                  pl.BlockSpec((ATTN_WIDTH, bn), lambda i, j: (0, j)),
                  pl.BlockSpec((POOL_WIDTH, bn), lambda i, j: (0, j)),
                  pl.BlockSpec((bm, bn), lambda i, j: (i, ga_col + j)),
                  pl.BlockSpec((bm, bn), lambda i, j: (i, gp_col + j)),
                  pl.BlockSpec((2, bn), lambda i, j: (0, j))],
        out_specs=pl.BlockSpec((bm, bn), lambda i, j: (i, j)),
        out_shape=jax.ShapeDtypeStruct((m, D_MODEL), jnp.bfloat16),
        compiler_params=_params(("parallel", "arbitrary")),
        name="gated_merge",
    )(y_attn, y_pool, wa, wp, proj, proj, gate_bias)


def _out_kernel(m_ref, w_ref, x_ref, g_ref, o_ref, *, bn, n_col):
    j = pl.program_id(1)
    acc = jnp.dot(m_ref[...], w_ref[...], preferred_element_type=jnp.float32) + x_ref[...]
    for jj in range(n_col):
        @pl.when(j == jj)
        def _():
            o_ref[:, jj * bn:(jj + 1) * bn] = acc

    @pl.when(j == n_col - 1)
    def _():
        y = o_ref[...]
        ms = jnp.mean(y * y, axis=-1, keepdims=True)
        o_ref[...] = y * lax.rsqrt(ms + EPS) * g_ref[...]


def _out_proj_norm(m_arr, w_out, x2, gain, bm=512, bn=512):
    m, d = x2.shape
    n_col = d // bn
    return pl.pallas_call(
        functools.partial(_out_kernel, bn=bn, n_col=n_col),
        grid=(m // bm, n_col),
        in_specs=[pl.BlockSpec((bm, d), lambda i, j: (i, 0)),
                  pl.BlockSpec((d, bn), lambda i, j: (0, j)),
                  pl.BlockSpec((bm, bn), lambda i, j: (i, j)),
                  pl.BlockSpec((1, d), lambda i, j: (0, 0))],
        out_specs=pl.BlockSpec((bm, d), lambda i, j: (i, 0)),
        out_shape=jax.ShapeDtypeStruct((m, d), jnp.float32),
        compiler_params=_params(("parallel", "arbitrary")),
        name="out_proj_norm",
    )(m_arr, w_out, x2, gain.reshape(1, d))


def kernel(x, norm_gain, w_in, rel_bias, pool_w, pool_scale, w_out_attn, w_out_pool,
           gate_bias, w_out, final_gain):
    batch, seq, d = x.shape
    x2 = x.reshape(batch * seq, d)
    bf = jnp.bfloat16

    h = _rmsnorm_bf16(x2, norm_gain)
    proj = _matmul_bf16(h, w_in.astype(bf))
    y_attn = _attention(proj, _band_bias(rel_bias), batch, seq)
    y_pool = _pooling(proj, pool_w, pool_scale, batch, seq)
    m = _merge(y_attn, y_pool, w_out_attn.astype(bf), w_out_pool.astype(bf), proj, gate_bias)
    out = _out_proj_norm(m, w_out.astype(bf), x2, final_gain)
    return out.reshape(batch, seq, d)
```

```python
import functools

import jax
import jax.numpy as jnp
from jax import lax
from jax.experimental import pallas as pl
from jax.experimental.pallas import tpu as pltpu

D_MODEL = 4096
CHUNK = 64
N_LEFT_CHUNKS = 8
ATTN_WIDTH = D_MODEL // 2
HEAD_DIM = 128
N_HEADS = ATTN_WIDTH // HEAD_DIM
MAX_REL = 128
POOL_WIDTH = D_MODEL // 2
POOL_WINDOWS = (2, 4, 8, 16)
POOL_GROUP_DIM = POOL_WIDTH // len(POOL_WINDOWS)
EPS = 1e-6

Q_OFF = 0
K_OFF = ATTN_WIDTH
V_OFF = 2 * ATTN_WIDTH
ZA_OFF = 3 * ATTN_WIDTH
U_OFF = 4 * ATTN_WIDTH
ZP_OFF = U_OFF + POOL_WIDTH
GA_OFF = ZP_OFF + POOL_WIDTH
GP_OFF = GA_OFF + D_MODEL
IN_COLS = GP_OFF + D_MODEL

Q_BLK = 2 * CHUNK
K_WIN = Q_BLK + N_LEFT_CHUNKS * CHUNK
K_BACK = N_LEFT_CHUNKS * CHUNK
HEADS_PER_STEP = 2
NEG = -1e30
LOG2E = 1.4426950408889634

POOL_TILE = 512
POOL_SUB = 128
NORM_ROWS = 32

VMEM_LIMIT = 56 * 1024 * 1024


def _params(sem):
    return pltpu.CompilerParams(dimension_semantics=sem, vmem_limit_bytes=VMEM_LIMIT)


def _norm_kernel(x_ref, g_ref, o_ref):
    x = x_ref[...]
    ms = jnp.mean(x * x, axis=-1, keepdims=True)
    o_ref[...] = (x * lax.rsqrt(ms + EPS) * g_ref[...]).astype(o_ref.dtype)


def _rmsnorm_bf16(x2, gain):
    m, d = x2.shape
    tm = 256
    return pl.pallas_call(
        _norm_kernel,
        grid=(m // tm,),
        in_specs=[pl.BlockSpec((tm, d), lambda i: (i, 0)),
                  pl.BlockSpec((1, d), lambda i: (0, 0))],
        out_specs=pl.BlockSpec((tm, d), lambda i: (i, 0)),
        out_shape=jax.ShapeDtypeStruct((m, d), jnp.bfloat16),
        compiler_params=_params(("parallel",)),
        name="rmsnorm_in",
    )(x2, gain.reshape(1, d))


def _mm_kernel(a_ref, b_ref, o_ref):
    o_ref[...] = jnp.dot(a_ref[...], b_ref[...],
                         preferred_element_type=jnp.float32).astype(o_ref.dtype)


def _matmul_bf16(a, b, bm=1024, bn=1024):
    m, k = a.shape
    _, n = b.shape
    return pl.pallas_call(
        _mm_kernel,
        grid=(m // bm, n // bn),
        in_specs=[pl.BlockSpec((bm, k), lambda i, j: (i, 0)),
                  pl.BlockSpec((k, bn), lambda i, j: (0, j))],
        out_specs=pl.BlockSpec((bm, bn), lambda i, j: (i, j)),
        out_shape=jax.ShapeDtypeStruct((m, n), jnp.bfloat16),
        compiler_params=_params(("parallel", "arbitrary")),
        name="in_proj",
    )(a, b)


def _silu(z):
    return z * jax.nn.sigmoid(z)


def _attn_kernel(q_ref, k_ref, v_ref, z_ref, bias_ref, o_ref, s_scr, p_scr, l_scr, *, seq):
    scale2 = HEAD_DIM ** -0.5 * LOG2E
    n_blk = seq // Q_BLK
    n_edge = K_BACK // Q_BLK
    heads = range(HEADS_PER_STEP)

    def rows(j):
        start = j * Q_BLK
        return pl.ds(start if isinstance(j, int) else pl.multiple_of(start, Q_BLK), Q_BLK)

    def key_rows(j):
        if isinstance(j, int):
            return pl.ds(max(j * Q_BLK - K_BACK, 0), K_WIN)
        return pl.ds(pl.multiple_of(j * Q_BLK - K_BACK, Q_BLK), K_WIN)

    def block_bias(hh, j):
        if isinstance(j, int) and j < n_edge:
            shift = K_BACK - j * Q_BLK
            pad = jnp.full((Q_BLK, shift), NEG, jnp.float32)
            return jnp.concatenate([bias_ref[hh, :, shift:], pad], axis=1)
        return bias_ref[hh]

    def scores(j, slot):
        for hh in heads:
            cs = slice(hh * HEAD_DIM, (hh + 1) * HEAD_DIM)
            s = lax.dot_general(q_ref[rows(j), cs], k_ref[key_rows(j), cs],
                                (((1,), (1,)), ((), ())), preferred_element_type=jnp.float32)
            s_scr[hh, slot] = s * scale2 + block_bias(hh, j)

    def softmax(j, slot):
        for hh in heads:
            s = s_scr[hh, slot]
            m = jnp.max(s, axis=-1, keepdims=True)
            p = jnp.exp2(s - m)
            l_scr[hh, slot] = jnp.sum(p, axis=-1, keepdims=True)
            p_scr[hh, slot] = p.astype(p_scr.dtype)

    def output(j, slot):
        for hh in heads:
            cs = slice(hh * HEAD_DIM, (hh + 1) * HEAD_DIM)
            o = jnp.dot(p_scr[hh, slot], v_ref[key_rows(j), cs],
                        preferred_element_type=jnp.float32)
            o = o / l_scr[hh, slot]
            z = z_ref[rows(j), cs].astype(jnp.float32)
            o_ref[rows(j), cs] = (o * _silu(z)).astype(o_ref.dtype)

    def step(t, parity, first=0, last=n_blk - 1):
        static = isinstance(t, int)
        if not static or first <= t - 2 <= last:
            output(t - 2, parity)
        if not static or first <= t - 1 <= last:
            softmax(t - 1, 1 - parity)
        if not static or first <= t <= last:
            scores(t, parity)

    n_head_steps = n_edge + 2
    assert n_head_steps % 2 == 0 and (n_blk - n_head_steps) % 2 == 0
    for t in range(n_head_steps):
        step(t, t % 2)

    def body(i, carry):
        t = n_head_steps + 2 * i
        step(t, 0)
        step(t + 1, 1)
        return carry

    lax.fori_loop(0, (n_blk - n_head_steps) // 2, body, 0)
    for t in range(n_blk, n_blk + 2):
        step(t, t % 2)


def _band_bias(rel_bias):
    n_heads = rel_bias.shape[0]
    width = K_WIN + Q_BLK + 1
    far = rel_bias[:, 2 * MAX_REL:]
    n_far_left = K_BACK - MAX_REL
    vec = jnp.concatenate(
        [jnp.broadcast_to(far, (n_heads, n_far_left)),
         rel_bias[:, :0:-1],
         jnp.broadcast_to(far, (n_heads, width - n_far_left - 2 * MAX_REL))], axis=1)
    skew = jnp.broadcast_to(vec[:, None, :], (n_heads, Q_BLK, width))
    skew = skew.reshape(n_heads, Q_BLK * width)[:, :Q_BLK * (width - 1)]
    table = skew.reshape(n_heads, Q_BLK, width - 1)[:, :, :K_WIN].astype(jnp.float32) * LOG2E
    qc = jnp.arange(Q_BLK)[:, None] // CHUNK
    kc = jnp.arange(K_WIN)[None, :] // CHUNK
    valid = (kc >= qc) & (kc <= qc + N_LEFT_CHUNKS)
    return jnp.where(valid[None], table, NEG)


def _attention(proj, bias, batch, seq):
    hw = HEADS_PER_STEP * HEAD_DIM
    n_hp = N_HEADS // HEADS_PER_STEP

    def col_spec(off):
        base = off // hw
        return pl.BlockSpec((seq, hw), lambda b, h: (b, base + h))

    return pl.pallas_call(
        functools.partial(_attn_kernel, seq=seq),
        grid=(batch, n_hp),
        in_specs=[col_spec(Q_OFF), col_spec(K_OFF), col_spec(V_OFF), col_spec(ZA_OFF),
                  pl.BlockSpec((HEADS_PER_STEP, Q_BLK, K_WIN), lambda b, h: (h, 0, 0))],
        out_specs=pl.BlockSpec((seq, hw), lambda b, h: (b, h)),
        out_shape=jax.ShapeDtypeStruct((batch * seq, ATTN_WIDTH), jnp.bfloat16),
        scratch_shapes=[pltpu.VMEM((HEADS_PER_STEP, 2, Q_BLK, K_WIN), jnp.float32),
                        pltpu.VMEM((HEADS_PER_STEP, 2, Q_BLK, K_WIN), jnp.bfloat16),
                        pltpu.VMEM((HEADS_PER_STEP, 2, Q_BLK, 1), jnp.float32)],
        compiler_params=_params(("parallel", "arbitrary")),
        name="band_attn",
    )(proj, proj, proj, proj, bias)


def _pool_kernel(u_ref, halo_ref, z_ref, a_ref, pw_ref, ps_ref, o_ref, buf_ref):
    s = pl.program_id(1)
    halo = halo_ref[...]
    halo = jnp.where(s == 0, jnp.zeros_like(halo), halo)
    buf_ref[0:POOL_SUB, :] = halo
    buf_ref[POOL_SUB:, :] = u_ref[...]
    t0 = s * POOL_TILE
    row = lax.broadcasted_iota(jnp.int32, (POOL_SUB, 1), 0)
    n_sub = POOL_TILE // POOL_SUB

    for g, w in enumerate(POOL_WINDOWS):
        cs = slice(g * POOL_GROUP_DIM, (g + 1) * POOL_GROUP_DIM)
        ds = []
        for j in range(n_sub):
            win = buf_ref[j * POOL_SUB:(j + 2) * POOL_SUB, cs]
            wsum = jnp.dot(a_ref[g], win, preferred_element_type=jnp.float32)
            t = t0 + j * POOL_SUB + row
            cnt = jnp.minimum(t + 1, w).astype(jnp.float32)
            cur = buf_ref[(j + 1) * POOL_SUB:(j + 2) * POOL_SUB, cs].astype(jnp.float32)
            ds.append((wsum / cnt - cur).astype(jnp.bfloat16))
        d = jnp.concatenate(ds, axis=0)
        y = jnp.dot(d, pw_ref[g], preferred_element_type=jnp.float32)
        z = z_ref[:, cs].astype(jnp.float32)
        o_ref[:, cs] = (y * ps_ref[:, cs] * _silu(z)).astype(o_ref.dtype)


def _window_matrices():
    r = jnp.arange(POOL_SUB)[:, None] + POOL_SUB
    c = jnp.arange(2 * POOL_SUB)[None, :]
    mats = [((c <= r) & (c > r - w)) for w in POOL_WINDOWS]
    return jnp.stack(mats).astype(jnp.bfloat16)


def _pooling(proj, pool_w, pool_scale, batch, seq):
    n_tiles = seq // POOL_TILE
    sub_per_tile = POOL_TILE // POOL_SUB
    sub_per_seq = seq // POOL_SUB
    u_col = U_OFF // POOL_WIDTH
    z_col = ZP_OFF // POOL_WIDTH
    g = len(POOL_WINDOWS)

    def halo_map(b, s):
        return (jnp.maximum(b * sub_per_seq + s * sub_per_tile - 1, 0), u_col)

    return pl.pallas_call(
        _pool_kernel,
        grid=(batch, n_tiles),
        in_specs=[pl.BlockSpec((POOL_TILE, POOL_WIDTH), lambda b, s: (b * n_tiles + s, u_col)),
                  pl.BlockSpec((POOL_SUB, POOL_WIDTH), halo_map),
                  pl.BlockSpec((POOL_TILE, POOL_WIDTH), lambda b, s: (b * n_tiles + s, z_col)),
                  pl.BlockSpec((g, POOL_SUB, 2 * POOL_SUB), lambda b, s: (0, 0, 0)),
                  pl.BlockSpec((g, POOL_GROUP_DIM, POOL_GROUP_DIM), lambda b, s: (0, 0, 0)),
                  pl.BlockSpec((1, POOL_WIDTH), lambda b, s: (0, 0))],
        out_specs=pl.BlockSpec((POOL_TILE, POOL_WIDTH), lambda b, s: (b * n_tiles + s, 0)),
        out_shape=jax.ShapeDtypeStruct((batch * seq, POOL_WIDTH), jnp.bfloat16),
        scratch_shapes=[pltpu.VMEM((POOL_TILE + POOL_SUB, POOL_WIDTH), jnp.bfloat16)],
        compiler_params=_params(("parallel", "arbitrary")),
        name="ms_pool",
    )(proj, proj, proj, _window_matrices(), pool_w.astype(jnp.bfloat16),
      pool_scale.reshape(1, POOL_WIDTH))


def _merge_kernel(ya_ref, yp_ref, wa_ref, wp_ref, ga_ref, gp_ref, gb_ref, o_ref):
    a = jnp.dot(ya_ref[...], wa_ref[...], preferred_element_type=jnp.float32)
    p = jnp.dot(yp_ref[...], wp_ref[...], preferred_element_type=jnp.float32)
    ga = jax.nn.sigmoid(ga_ref[...].astype(jnp.float32) + gb_ref[0:1, :])
    gp = jax.nn.sigmoid(gp_ref[...].astype(jnp.float32) + gb_ref[1:2, :])
    o_ref[...] = (ga * a + gp * p).astype(o_ref.dtype)


def _merge(y_attn, y_pool, wa, wp, proj, gate_bias, bm=1024, bn=1024):
    m = y_attn.shape[0]
    ga_col = GA_OFF // bn
    gp_col = GP_OFF // bn
    return pl.pallas_call(
        _merge_kernel,
        grid=(m // bm, D_MODEL // bn),
        in_specs=[pl.BlockSpec((bm, ATTN_WIDTH), lambda i, j: (i, 0)),
                  pl.BlockSpec((bm, POOL_WIDTH), lambda i, j: (i, 0)),
                  pl.BlockSpec((ATTN_WIDTH, bn), lambda i, j: (0, j)),
                  pl.BlockSpec((POOL_WIDTH, bn), lambda i, j: (0, j)),
                  pl.BlockSpec((bm, bn), lambda i, j: (i, ga_col + j)),
                  pl.BlockSpec((bm, bn), lambda i, j: (i, gp_col + j)),
                  pl.BlockSpec((2, bn), lambda i, j: (0, j))],
        out_specs=pl.BlockSpec((bm, bn), lambda i, j: (i, j)),
        out_shape=jax.ShapeDtypeStruct((m, D_MODEL), jnp.bfloat16),
        compiler_params=_params(("parallel", "arbitrary")),
        name="gated_merge",
    )(y_attn, y_pool, wa, wp, proj, proj, gate_bias)


def _out_kernel(m_ref, w_ref, x_ref, g_ref, o_ref, *, bn, n_col):
    j = pl.program_id(1)
    acc = jnp.dot(m_ref[...], w_ref[...], preferred_element_type=jnp.float32) + x_ref[...]
    for jj in range(n_col):
        @pl.when(j == jj)
        def _():
            o_ref[:, jj * bn:(jj + 1) * bn] = acc

    @pl.when(j == n_col - 1)
    def _():
        def norm_rows(r, carry):
            rows = pl.ds(pl.multiple_of(r * NORM_ROWS, NORM_ROWS), NORM_ROWS)
            y = o_ref[rows, :]
            ms = jnp.mean(y * y, axis=-1, keepdims=True)
            o_ref[rows, :] = y * lax.rsqrt(ms + EPS) * g_ref[...]
            return carry

        lax.fori_loop(0, o_ref.shape[0] // NORM_ROWS, norm_rows, 0)


def _out_proj_norm(m_arr, w_out, x2, gain, bm=1024, bn=512):
    m, d = x2.shape
    n_col = d // bn
    return pl.pallas_call(
        functools.partial(_out_kernel, bn=bn, n_col=n_col),
        grid=(m // bm, n_col),
        in_specs=[pl.BlockSpec((bm, d), lambda i, j: (i, 0)),
                  pl.BlockSpec((d, bn), lambda i, j: (0, j)),
                  pl.BlockSpec((bm, bn), lambda i, j: (i, j)),
                  pl.BlockSpec((1, d), lambda i, j: (0, 0))],
        out_specs=pl.BlockSpec((bm, d), lambda i, j: (i, 0), pipeline_mode=pl.Buffered(1)),
        out_shape=jax.ShapeDtypeStruct((m, d), jnp.float32),
        compiler_params=_params(("parallel", "arbitrary")),
        name="out_proj_norm",
    )(m_arr, w_out, x2, gain.reshape(1, d))


def kernel(x, norm_gain, w_in, rel_bias, pool_w, pool_scale, w_out_attn, w_out_pool,
           gate_bias, w_out, final_gain):
    batch, seq, d = x.shape
    x2 = x.reshape(batch * seq, d)
    bf = jnp.bfloat16

    h = _rmsnorm_bf16(x2, norm_gain)
    proj = _matmul_bf16(h, w_in.astype(bf))
    y_attn = _attention(proj, _band_bias(rel_bias), batch, seq)
    y_pool = _pooling(proj, pool_w, pool_scale, batch, seq)
    m = _merge(y_attn, y_pool, w_out_attn.astype(bf), w_out_pool.astype(bf), proj, gate_bias)
    out = _out_proj_norm(m, w_out.astype(bf), x2, final_gain)
    return out.reshape(batch, seq, d)
```

```python
import functools

import jax
import jax.numpy as jnp
from jax import lax
from jax.experimental import pallas as pl
from jax.experimental.pallas import tpu as pltpu

D_MODEL = 4096
CHUNK = 64
N_LEFT_CHUNKS = 8
ATTN_WIDTH = D_MODEL // 2
HEAD_DIM = 128
N_HEADS = ATTN_WIDTH // HEAD_DIM
MAX_REL = 128
POOL_WIDTH = D_MODEL // 2
POOL_WINDOWS = (2, 4, 8, 16)
POOL_GROUP_DIM = POOL_WIDTH // len(POOL_WINDOWS)
EPS = 1e-6

Q_OFF = 0
K_OFF = ATTN_WIDTH
V_OFF = 2 * ATTN_WIDTH
ZA_OFF = 3 * ATTN_WIDTH
U_OFF = 4 * ATTN_WIDTH
ZP_OFF = U_OFF + POOL_WIDTH
GA_OFF = ZP_OFF + POOL_WIDTH
GP_OFF = GA_OFF + D_MODEL
IN_COLS = GP_OFF + D_MODEL

Q_BLK = 2 * CHUNK
K_WIN = Q_BLK + N_LEFT_CHUNKS * CHUNK
K_BACK = N_LEFT_CHUNKS * CHUNK
HEADS_PER_STEP = 2
NEG = -1e30
LOG2E = 1.4426950408889634

POOL_TILE = 512
POOL_SUB = 128
NORM_ROWS = 128
LANES = 128
NORM_CHUNK = 64

VMEM_LIMIT = 56 * 1024 * 1024


def _params(sem):
    return pltpu.CompilerParams(dimension_semantics=sem, vmem_limit_bytes=VMEM_LIMIT)


def _in_proj_kernel(x_ref, g_ref, w_ref, o_ref, h_scr, *, n_row):
    i = pl.program_id(0)
    j = pl.program_id(1)
    n_chunks = h_scr.shape[1] // NORM_CHUNK

    def norm_chunk(slot):
        x = x_ref[...]
        ms = jnp.mean(x * x, axis=-1, keepdims=True)
        h = (x * lax.rsqrt(ms + EPS) * g_ref[...]).astype(h_scr.dtype)
        c = jnp.minimum(j, n_chunks - 1)
        h_scr[slot, pl.ds(pl.multiple_of(c * NORM_CHUNK, NORM_CHUNK), NORM_CHUNK), :] = h

    def matmul(slot):
        o_ref[...] = jnp.dot(h_scr[slot], w_ref[...],
                             preferred_element_type=jnp.float32).astype(o_ref.dtype)

    @pl.when(i == 0)
    def _():
        norm_chunk(0)

    for parity in (0, 1):
        @pl.when((i > 0) & (i < n_row) & (i % 2 == parity))
        def _():
            matmul(1 - parity)
            norm_chunk(parity)

    @pl.when(i == n_row)
    def _():
        matmul((n_row - 1) % 2)


def _norm_in_proj(x2, gain, w, bm=1024, bn=1024):
    m, d = x2.shape
    _, n = w.shape
    n_row = m // bm
    n_chunks = bm // NORM_CHUNK
    assert n // bn >= n_chunks

    def x_map(i, j):
        chunk = jnp.where(i >= n_row, n_chunks - 1, jnp.minimum(j, n_chunks - 1))
        return (jnp.minimum(i, n_row - 1) * n_chunks + chunk, 0)

    def col(i, j):
        return jnp.where(i == 0, 0, j)

    return pl.pallas_call(
        functools.partial(_in_proj_kernel, n_row=n_row),
        grid=(n_row + 1, n // bn),
        in_specs=[pl.BlockSpec((NORM_CHUNK, d), x_map),
                  pl.BlockSpec((1, d), lambda i, j: (0, 0)),
                  pl.BlockSpec((d, bn), lambda i, j: (0, col(i, j)))],
        out_specs=pl.BlockSpec((bm, bn), lambda i, j: (jnp.maximum(i - 1, 0), col(i, j))),
        out_shape=jax.ShapeDtypeStruct((m, n), jnp.bfloat16),
        scratch_shapes=[pltpu.VMEM((2, bm, d), jnp.bfloat16)],
        compiler_params=_params(("arbitrary", "arbitrary")),
        name="in_proj",
    )(x2, gain.reshape(1, d), w)


def _silu(z):
    return z * jax.nn.sigmoid(z)


def _attn_kernel(q_ref, k_ref, v_ref, z_ref, bias_ref, o_ref, s_scr, p_scr, l_scr, *, seq):
    scale2 = HEAD_DIM ** -0.5 * LOG2E
    n_blk = seq // Q_BLK
    n_edge = K_BACK // Q_BLK
    heads = range(HEADS_PER_STEP)

    def rows(j):
        start = j * Q_BLK
        return pl.ds(start if isinstance(j, int) else pl.multiple_of(start, Q_BLK), Q_BLK)

    def key_rows(j):
        if isinstance(j, int):
            return pl.ds(max(j * Q_BLK - K_BACK, 0), K_WIN)
        return pl.ds(pl.multiple_of(j * Q_BLK - K_BACK, Q_BLK), K_WIN)

    def block_bias(hh, j):
        if isinstance(j, int) and j < n_edge:
            shift = K_BACK - j * Q_BLK
            pad = jnp.full((Q_BLK, shift), NEG, jnp.float32)
            return jnp.concatenate([bias_ref[hh, :, shift:], pad], axis=1)
        return bias_ref[hh]

    def scores(j, slot):
        for hh in heads:
            cs = slice(hh * HEAD_DIM, (hh + 1) * HEAD_DIM)
            s = lax.dot_general(q_ref[rows(j), cs], k_ref[key_rows(j), cs],
                                (((1,), (1,)), ((), ())), preferred_element_type=jnp.float32)
            s_scr[hh, slot] = s * scale2 + block_bias(hh, j)

    def softmax(j, slot):
        for hh in heads:
            s = s_scr[hh, slot]
            m = jnp.max(s, axis=-1, keepdims=True)
            p = jnp.exp2(s - m)
            l_scr[hh, slot] = jnp.sum(p, axis=-1, keepdims=True)
            p_scr[hh, slot] = p.astype(p_scr.dtype)

    def output(j, slot):
        for hh in heads:
            cs = slice(hh * HEAD_DIM, (hh + 1) * HEAD_DIM)
            o = jnp.dot(p_scr[hh, slot], v_ref[key_rows(j), cs],
                        preferred_element_type=jnp.float32)
            o = o / l_scr[hh, slot]
            z = z_ref[rows(j), cs].astype(jnp.float32)
            o_ref[rows(j), cs] = (o * _silu(z)).astype(o_ref.dtype)

    def step(t, parity, first=0, last=n_blk - 1):
        static = isinstance(t, int)
        if not static or first <= t - 2 <= last:
            output(t - 2, parity)
        if not static or first <= t - 1 <= last:
            softmax(t - 1, 1 - parity)
        if not static or first <= t <= last:
            scores(t, parity)

    n_head_steps = n_edge + 2
    assert n_head_steps % 2 == 0 and (n_blk - n_head_steps) % 2 == 0
    for t in range(n_head_steps):
        step(t, t % 2)

    def body(i, carry):
        t = n_head_steps + 2 * i
        step(t, 0)
        step(t + 1, 1)
        return carry

    lax.fori_loop(0, (n_blk - n_head_steps) // 2, body, 0)
    for t in range(n_blk, n_blk + 2):
        step(t, t % 2)


def _band_bias(rel_bias):
    n_heads = rel_bias.shape[0]
    width = K_WIN + Q_BLK + 1
    far = rel_bias[:, 2 * MAX_REL:]
    n_far_left = K_BACK - MAX_REL
    vec = jnp.concatenate(
        [jnp.broadcast_to(far, (n_heads, n_far_left)),
         rel_bias[:, :0:-1],
         jnp.broadcast_to(far, (n_heads, width - n_far_left - 2 * MAX_REL))], axis=1)
    skew = jnp.broadcast_to(vec[:, None, :], (n_heads, Q_BLK, width))
    skew = skew.reshape(n_heads, Q_BLK * width)[:, :Q_BLK * (width - 1)]
    table = skew.reshape(n_heads, Q_BLK, width - 1)[:, :, :K_WIN].astype(jnp.float32) * LOG2E
    qc = jnp.arange(Q_BLK)[:, None] // CHUNK
    kc = jnp.arange(K_WIN)[None, :] // CHUNK
    valid = (kc >= qc) & (kc <= qc + N_LEFT_CHUNKS)
    return jnp.where(valid[None], table, NEG)


def _attention(proj, bias, batch, seq):
    hw = HEADS_PER_STEP * HEAD_DIM
    n_hp = N_HEADS // HEADS_PER_STEP

    def col_spec(off):
        base = off // hw
        return pl.BlockSpec((seq, hw), lambda b, h: (b, base + h))

    return pl.pallas_call(
        functools.partial(_attn_kernel, seq=seq),
        grid=(batch, n_hp),
        in_specs=[col_spec(Q_OFF), col_spec(K_OFF), col_spec(V_OFF), col_spec(ZA_OFF),
                  pl.BlockSpec((HEADS_PER_STEP, Q_BLK, K_WIN), lambda b, h: (h, 0, 0))],
        out_specs=pl.BlockSpec((seq, hw), lambda b, h: (b, h)),
        out_shape=jax.ShapeDtypeStruct((batch * seq, ATTN_WIDTH), jnp.bfloat16),
        scratch_shapes=[pltpu.VMEM((HEADS_PER_STEP, 2, Q_BLK, K_WIN), jnp.float32),
                        pltpu.VMEM((HEADS_PER_STEP, 2, Q_BLK, K_WIN), jnp.bfloat16),
                        pltpu.VMEM((HEADS_PER_STEP, 2, Q_BLK, 1), jnp.float32)],
        compiler_params=_params(("parallel", "arbitrary")),
        name="band_attn",
    )(proj, proj, proj, proj, bias)


def _pool_kernel(u_ref, halo_ref, z_ref, a_ref, pw_ref, ps_ref, o_ref, buf_ref):
    s = pl.program_id(1)
    halo = halo_ref[...]
    halo = jnp.where(s == 0, jnp.zeros_like(halo), halo)
    buf_ref[0:POOL_SUB, :] = halo
    buf_ref[POOL_SUB:, :] = u_ref[...]
    t0 = s * POOL_TILE
    row = lax.broadcasted_iota(jnp.int32, (POOL_SUB, 1), 0)
    n_sub = POOL_TILE // POOL_SUB

    for g, w in enumerate(POOL_WINDOWS):
        cs = slice(g * POOL_GROUP_DIM, (g + 1) * POOL_GROUP_DIM)
        ds = []
        for j in range(n_sub):
            win = buf_ref[j * POOL_SUB:(j + 2) * POOL_SUB, cs]
            wsum = jnp.dot(a_ref[g], win, preferred_element_type=jnp.float32)
            t = t0 + j * POOL_SUB + row
            cnt = jnp.minimum(t + 1, w).astype(jnp.float32)
            cur = buf_ref[(j + 1) * POOL_SUB:(j + 2) * POOL_SUB, cs].astype(jnp.float32)
            ds.append((wsum / cnt - cur).astype(jnp.bfloat16))
        d = jnp.concatenate(ds, axis=0)
        y = jnp.dot(d, pw_ref[g], preferred_element_type=jnp.float32)
        z = z_ref[:, cs].astype(jnp.float32)
        o_ref[:, cs] = (y * ps_ref[:, cs] * _silu(z)).astype(o_ref.dtype)


def _window_matrices():
    r = jnp.arange(POOL_SUB)[:, None] + POOL_SUB
    c = jnp.arange(2 * POOL_SUB)[None, :]
    mats = [((c <= r) & (c > r - w)) for w in POOL_WINDOWS]
    return jnp.stack(mats).astype(jnp.bfloat16)


def _pooling(proj, pool_w, pool_scale, batch, seq):
    n_tiles = seq // POOL_TILE
    sub_per_tile = POOL_TILE // POOL_SUB
    sub_per_seq = seq // POOL_SUB
    u_col = U_OFF // POOL_WIDTH
    z_col = ZP_OFF // POOL_WIDTH
    g = len(POOL_WINDOWS)

    def halo_map(b, s):
        return (jnp.maximum(b * sub_per_seq + s * sub_per_tile - 1, 0), u_col)

    return pl.pallas_call(
        _pool_kernel,
        grid=(batch, n_tiles),
        in_specs=[pl.BlockSpec((POOL_TILE, POOL_WIDTH), lambda b, s: (b * n_tiles + s, u_col)),
                  pl.BlockSpec((POOL_SUB, POOL_WIDTH), halo_map),
                  pl.BlockSpec((POOL_TILE, POOL_WIDTH), lambda b, s: (b * n_tiles + s, z_col)),
                  pl.BlockSpec((g, POOL_SUB, 2 * POOL_SUB), lambda b, s: (0, 0, 0)),
                  pl.BlockSpec((g, POOL_GROUP_DIM, POOL_GROUP_DIM), lambda b, s: (0, 0, 0)),
                  pl.BlockSpec((1, POOL_WIDTH), lambda b, s: (0, 0))],
        out_specs=pl.BlockSpec((POOL_TILE, POOL_WIDTH), lambda b, s: (b * n_tiles + s, 0)),
        out_shape=jax.ShapeDtypeStruct((batch * seq, POOL_WIDTH), jnp.bfloat16),
        scratch_shapes=[pltpu.VMEM((POOL_TILE + POOL_SUB, POOL_WIDTH), jnp.bfloat16)],
        compiler_params=_params(("parallel", "arbitrary")),
        name="ms_pool",
    )(proj, proj, proj, _window_matrices(), pool_w.astype(jnp.bfloat16),
      pool_scale.reshape(1, POOL_WIDTH))


def _merge_kernel(ya_ref, yp_ref, wa_ref, wp_ref, ga_ref, gp_ref, gb_ref, o_ref):
    a = jnp.dot(ya_ref[...], wa_ref[...], preferred_element_type=jnp.float32)
    p = jnp.dot(yp_ref[...], wp_ref[...], preferred_element_type=jnp.float32)
    ga = jax.nn.sigmoid(ga_ref[...].astype(jnp.float32) + gb_ref[0:1, :])
    gp = jax.nn.sigmoid(gp_ref[...].astype(jnp.float32) + gb_ref[1:2, :])
    o_ref[...] = (ga * a + gp * p).astype(o_ref.dtype)


def _merge(y_attn, y_pool, wa, wp, proj, gate_bias, bm=1024, bn=1024):
    m = y_attn.shape[0]
    ga_col = GA_OFF // bn
    gp_col = GP_OFF // bn
    return pl.pallas_call(
        _merge_kernel,
        grid=(m // bm, D_MODEL // bn),
        in_specs=[pl.BlockSpec((bm, ATTN_WIDTH), lambda i, j: (i, 0)),
                  pl.BlockSpec((bm, POOL_WIDTH), lambda i, j: (i, 0)),
                  pl.BlockSpec((ATTN_WIDTH, bn), lambda i, j: (0, j)),
                  pl.BlockSpec((POOL_WIDTH, bn), lambda i, j: (0, j)),
                  pl.BlockSpec((bm, bn), lambda i, j: (i, ga_col + j)),
                  pl.BlockSpec((bm, bn), lambda i, j: (i, gp_col + j)),
                  pl.BlockSpec((2, bn), lambda i, j: (0, j))],
        out_specs=pl.BlockSpec((bm, bn), lambda i, j: (i, j)),
        out_shape=jax.ShapeDtypeStruct((m, D_MODEL), jnp.bfloat16),
        compiler_params=_params(("parallel", "arbitrary")),
        name="gated_merge",
    )(y_attn, y_pool, wa, wp, proj, proj, gate_bias)


def _out_kernel(m_ref, w_ref, x_ref, g_ref, o_hbm, r_scr, ssq_scr, scale_scr, sem, *, n_row, n_col):
    i = pl.program_id(0)
    j = pl.program_id(1)
    _, bm, bn = r_scr.shape
    d = n_col * bn

    def out_copy(row_blk, col_blk):
        return pltpu.make_async_copy(
            r_scr.at[col_blk],
            o_hbm.at[pl.ds(row_blk * bm, bm), pl.ds(col_blk * bn, bn)],
            sem.at[col_blk])

    @pl.when(i > 0)
    def _():
        out_copy(i - 1, j).wait()

    acc = jnp.dot(m_ref[...], w_ref[...], preferred_element_type=jnp.float32) + x_ref[...]
    r_scr[j] = acc
    sq = acc * acc
    part = sq[:, 0:LANES]
    for c in range(1, bn // LANES):
        part = part + sq[:, c * LANES:(c + 1) * LANES]

    @pl.when(j == 0)
    def _():
        ssq_scr[...] = part

    @pl.when(j > 0)
    def _():
        ssq_scr[...] += part

    @pl.when(j == n_col - 1)
    def _():
        ms = jnp.sum(ssq_scr[...], axis=-1, keepdims=True) * (1.0 / d)
        scale_scr[...] = jnp.broadcast_to(lax.rsqrt(ms + EPS), scale_scr.shape)
        for jj in range(n_col):
            def scale_rows(r, carry):
                rows = pl.ds(pl.multiple_of(r * NORM_ROWS, NORM_ROWS), NORM_ROWS)
                sc = scale_scr[rows, :]
                for c in range(bn // LANES):
                    cols = slice(c * LANES, (c + 1) * LANES)
                    gain = g_ref[:, jj * bn + c * LANES:jj * bn + (c + 1) * LANES]
                    r_scr[jj, rows, cols] = r_scr[jj, rows, cols] * sc * gain
                return carry

            lax.fori_loop(0, bm // NORM_ROWS, scale_rows, 0)
            out_copy(i, jj).start()

        @pl.when(i == n_row - 1)
        def _():
            for jj in range(n_col):
                out_copy(i, jj).wait()


def _out_proj_norm(m_arr, w_out, x2, gain, bm=1024, bn=512):
    m, d = x2.shape
    n_row, n_col = m // bm, d // bn
    return pl.pallas_call(
        functools.partial(_out_kernel, n_row=n_row, n_col=n_col),
        grid=(n_row, n_col),
        in_specs=[pl.BlockSpec((bm, d), lambda i, j: (i, 0)),
                  pl.BlockSpec((d, bn), lambda i, j: (0, j)),
                  pl.BlockSpec((bm, bn), lambda i, j: (i, j)),
                  pl.BlockSpec((1, d), lambda i, j: (0, 0))],
        out_specs=pl.BlockSpec(memory_space=pl.ANY),
        out_shape=jax.ShapeDtypeStruct((m, d), jnp.float32),
        scratch_shapes=[pltpu.VMEM((n_col, bm, bn), jnp.float32),
                        pltpu.VMEM((bm, LANES), jnp.float32),
                        pltpu.VMEM((bm, LANES), jnp.float32),
                        pltpu.SemaphoreType.DMA((n_col,))],
        compiler_params=_params(("arbitrary", "arbitrary")),
        name="out_proj_norm",
    )(m_arr, w_out, x2, gain.reshape(1, d))


def kernel(x, norm_gain, w_in, rel_bias, pool_w, pool_scale, w_out_attn, w_out_pool,
           gate_bias, w_out, final_gain):
    batch, seq, d = x.shape
    x2 = x.reshape(batch * seq, d)
    bf = jnp.bfloat16

    proj = _norm_in_proj(x2, norm_gain, w_in.astype(bf))
    y_attn = _attention(proj, _band_bias(rel_bias), batch, seq)
    y_pool = _pooling(proj, pool_w, pool_scale, batch, seq)
    m = _merge(y_attn, y_pool, w_out_attn.astype(bf), w_out_pool.astype(bf), proj, gate_bias)
    out = _out_proj_norm(m, w_out.astype(bf), x2, final_gain)
    return out.reshape(batch, seq, d)
```

```python
import functools

import jax
import jax.numpy as jnp
from jax import lax
from jax.experimental import pallas as pl
from jax.experimental.pallas import tpu as pltpu

D_MODEL = 4096
CHUNK = 64
N_LEFT_CHUNKS = 8
ATTN_WIDTH = D_MODEL // 2
HEAD_DIM = 128
N_HEADS = ATTN_WIDTH // HEAD_DIM
MAX_REL = 128
POOL_WIDTH = D_MODEL // 2
POOL_WINDOWS = (2, 4, 8, 16)
POOL_GROUP_DIM = POOL_WIDTH // len(POOL_WINDOWS)
EPS = 1e-6

Q_OFF = 0
K_OFF = ATTN_WIDTH
V_OFF = 2 * ATTN_WIDTH
ZA_OFF = 3 * ATTN_WIDTH
U_OFF = 4 * ATTN_WIDTH
ZP_OFF = U_OFF + POOL_WIDTH
GA_OFF = ZP_OFF + POOL_WIDTH
GP_OFF = GA_OFF + D_MODEL
IN_COLS = GP_OFF + D_MODEL

Q_BLK = 2 * CHUNK
K_WIN = Q_BLK + N_LEFT_CHUNKS * CHUNK
K_BACK = N_LEFT_CHUNKS * CHUNK
ROW_BLK = K_BACK
HEADS_PER_STEP = 4
MERGE_COLS = 256
NEG = -1e30
LOG2E = 1.4426950408889634
SCORE_SCALE = HEAD_DIM ** -0.5 * LOG2E

POOL_TILE = 512
POOL_SUB = 128
NORM_ROWS = 128
LANES = 128
NORM_CHUNK = 64

VMEM_LIMIT = 56 * 1024 * 1024


def _params(sem):
    return pltpu.CompilerParams(dimension_semantics=sem, vmem_limit_bytes=VMEM_LIMIT)


def _in_proj_kernel(x_ref, g_ref, w_ref, cs_ref, o_ref, h_scr, *, n_row):
    i = pl.program_id(0)
    j = pl.program_id(1)
    n_chunks = h_scr.shape[1] // NORM_CHUNK

    def norm_chunk(slot):
        x = x_ref[...]
        ms = jnp.mean(x * x, axis=-1, keepdims=True)
        h = (x * lax.rsqrt(ms + EPS) * g_ref[...]).astype(h_scr.dtype)
        c = jnp.minimum(j, n_chunks - 1)
        h_scr[slot, pl.ds(pl.multiple_of(c * NORM_CHUNK, NORM_CHUNK), NORM_CHUNK), :] = h

    def matmul(slot):
        acc = jnp.dot(h_scr[slot], w_ref[...], preferred_element_type=jnp.float32)
        o_ref[...] = (acc * cs_ref[...]).astype(o_ref.dtype)

    @pl.when(i == 0)
    def _():
        norm_chunk(0)

    for parity in (0, 1):
        @pl.when((i > 0) & (i < n_row) & (i % 2 == parity))
        def _():
            matmul(1 - parity)
            norm_chunk(parity)

    @pl.when(i == n_row)
    def _():
        matmul((n_row - 1) % 2)


def _norm_in_proj(x2, gain, w, col_scale, bm=1024, bn=1024):
    m, d = x2.shape
    _, n = w.shape
    n_row = m // bm
    n_chunks = bm // NORM_CHUNK
    assert n // bn >= n_chunks

    def x_map(i, j):
        chunk = jnp.where(i >= n_row, n_chunks - 1, jnp.minimum(j, n_chunks - 1))
        return (jnp.minimum(i, n_row - 1) * n_chunks + chunk, 0)

    def col(i, j):
        return jnp.where(i == 0, 0, j)

    return pl.pallas_call(
        functools.partial(_in_proj_kernel, n_row=n_row),
        grid=(n_row + 1, n // bn),
        in_specs=[pl.BlockSpec((NORM_CHUNK, d), x_map),
                  pl.BlockSpec((1, d), lambda i, j: (0, 0)),
                  pl.BlockSpec((d, bn), lambda i, j: (0, col(i, j))),
                  pl.BlockSpec((1, bn), lambda i, j: (0, col(i, j)))],
        out_specs=pl.BlockSpec((bm, bn), lambda i, j: (jnp.maximum(i - 1, 0), col(i, j))),
        out_shape=jax.ShapeDtypeStruct((m, n), jnp.bfloat16),
        scratch_shapes=[pltpu.VMEM((2, bm, d), jnp.bfloat16)],
        compiler_params=_params(("arbitrary", "arbitrary")),
        name="in_proj",
    )(x2, gain.reshape(1, d), w, col_scale.reshape(1, n))


def _silu(z):
    return z * jax.nn.sigmoid(z)


def _pool_kernel(u_ref, halo_ref, z_ref, a_ref, pw_ref, ps_ref, o_ref, buf_ref):
    s = pl.program_id(1)
    halo = halo_ref[...]
    halo = jnp.where(s == 0, jnp.zeros_like(halo), halo)
    buf_ref[0:POOL_SUB, :] = halo
    buf_ref[POOL_SUB:, :] = u_ref[...]
    t0 = s * POOL_TILE
    row = lax.broadcasted_iota(jnp.int32, (POOL_SUB, 1), 0)
    n_sub = POOL_TILE // POOL_SUB

    for g, w in enumerate(POOL_WINDOWS):
        cs = slice(g * POOL_GROUP_DIM, (g + 1) * POOL_GROUP_DIM)
        ds = []
        for j in range(n_sub):
            win = buf_ref[j * POOL_SUB:(j + 2) * POOL_SUB, cs]
            wsum = jnp.dot(a_ref[g], win, preferred_element_type=jnp.float32)
            t = t0 + j * POOL_SUB + row
            cnt = jnp.minimum(t + 1, w).astype(jnp.float32)
            cur = buf_ref[(j + 1) * POOL_SUB:(j + 2) * POOL_SUB, cs].astype(jnp.float32)
            ds.append((wsum / cnt - cur).astype(jnp.bfloat16))
        d = jnp.concatenate(ds, axis=0)
        y = jnp.dot(d, pw_ref[g], preferred_element_type=jnp.float32)
        z = z_ref[:, cs].astype(jnp.float32)
        o_ref[:, cs] = (y * ps_ref[:, cs] * _silu(z)).astype(o_ref.dtype)


def _window_matrices():
    r = jnp.arange(POOL_SUB)[:, None] + POOL_SUB
    c = jnp.arange(2 * POOL_SUB)[None, :]
    mats = [((c <= r) & (c > r - w)) for w in POOL_WINDOWS]
    return jnp.stack(mats).astype(jnp.bfloat16)


def _pooling(proj, pool_w, pool_scale, batch, seq):
    n_tiles = seq // POOL_TILE
    sub_per_tile = POOL_TILE // POOL_SUB
    sub_per_seq = seq // POOL_SUB
    u_col = U_OFF // POOL_WIDTH
    z_col = ZP_OFF // POOL_WIDTH
    g = len(POOL_WINDOWS)

    def halo_map(b, s):
        return (jnp.maximum(b * sub_per_seq + s * sub_per_tile - 1, 0), u_col)

    return pl.pallas_call(
        _pool_kernel,
        grid=(batch, n_tiles),
        in_specs=[pl.BlockSpec((POOL_TILE, POOL_WIDTH), lambda b, s: (b * n_tiles + s, u_col)),
                  pl.BlockSpec((POOL_SUB, POOL_WIDTH), halo_map),
                  pl.BlockSpec((POOL_TILE, POOL_WIDTH), lambda b, s: (b * n_tiles + s, z_col)),
                  pl.BlockSpec((g, POOL_SUB, 2 * POOL_SUB), lambda b, s: (0, 0, 0)),
                  pl.BlockSpec((g, POOL_GROUP_DIM, POOL_GROUP_DIM), lambda b, s: (0, 0, 0)),
                  pl.BlockSpec((1, POOL_WIDTH), lambda b, s: (0, 0))],
        out_specs=pl.BlockSpec((POOL_TILE, POOL_WIDTH), lambda b, s: (b * n_tiles + s, 0)),
        out_shape=jax.ShapeDtypeStruct((batch * seq, POOL_WIDTH), jnp.bfloat16),
        scratch_shapes=[pltpu.VMEM((POOL_TILE + POOL_SUB, POOL_WIDTH), jnp.bfloat16)],
        compiler_params=_params(("parallel", "arbitrary")),
        name="ms_pool",
    )(proj, proj, proj, _window_matrices(), pool_w.astype(jnp.bfloat16),
      pool_scale.reshape(1, POOL_WIDTH))


def _attn_merge_kernel(q_ref, z_ref, kh_ref, km_ref, vh_ref, vm_ref, bias_ref,
                       yp_ref, wa_ref, wp_ref, ga_ref, gp_ref, gb_ref, o_ref,
                       ya_even, ya_odd, kbuf, vbuf, s_scr, p_scr, l_scr, *, n_row, blocks_per_seq):
    t = pl.program_id(0)
    j = pl.program_id(1)
    n_qb = ROW_BLK // Q_BLK
    n_piece = o_ref.shape[1] // MERGE_COLS

    def head_cols(hh):
        return slice(hh * HEAD_DIM, (hh + 1) * HEAD_DIM)

    def load_windows():
        kbuf[0:ROW_BLK, :] = kh_ref[...]
        kbuf[ROW_BLK:, :] = km_ref[...]
        vbuf[0:ROW_BLK, :] = vh_ref[...]
        vbuf[ROW_BLK:, :] = vm_ref[...]

    def scores_softmax(i):
        seq_start = (t % blocks_per_seq) == 0
        col = lax.broadcasted_iota(jnp.int32, (Q_BLK, K_WIN), 1)
        dead = seq_start & (col < ROW_BLK - i * Q_BLK)
        for hh in range(HEADS_PER_STEP):
            cs = head_cols(hh)
            s = lax.dot_general(q_ref[i * Q_BLK:(i + 1) * Q_BLK, cs],
                                kbuf[i * Q_BLK:i * Q_BLK + K_WIN, cs],
                                (((1,), (1,)), ((), ())), preferred_element_type=jnp.float32)
            s_scr[hh, i] = s + jnp.where(dead, NEG, bias_ref[hh])
        for hh in range(HEADS_PER_STEP):
            s = s_scr[hh, i]
            m = jnp.max(s, axis=-1, keepdims=True)
            p = jnp.exp2(s - m)
            l_scr[hh, i] = jnp.sum(p, axis=-1, keepdims=True)
            p_scr[hh, i] = p.astype(p_scr.dtype)

    def attn_output(i, ya_w):
        for hh in range(HEADS_PER_STEP):
            cs = head_cols(hh)
            o = jnp.dot(p_scr[hh, i], vbuf[i * Q_BLK:i * Q_BLK + K_WIN, cs],
                        preferred_element_type=jnp.float32)
            z = z_ref[i * Q_BLK:(i + 1) * Q_BLK, cs].astype(jnp.float32)
            y = o * z / ((1.0 + jnp.exp(-z)) * l_scr[hh, i])
            ya_w[j, i * Q_BLK:(i + 1) * Q_BLK, cs] = y.astype(ya_w.dtype)

    def merge_pieces(ya_r):
        thunks = []
        for c in range(n_piece):
            cols = slice(c * MERGE_COLS, (c + 1) * MERGE_COLS)
            held = {}

            def attn_branch(cols=cols, held=held):
                ya = jnp.concatenate([ya_r[g] for g in range(ya_r.shape[0])], axis=1)
                a = jnp.dot(ya, wa_ref[:, cols], preferred_element_type=jnp.float32)
                ga = jax.nn.sigmoid(ga_ref[:, cols].astype(jnp.float32) + gb_ref[0:1, cols])
                held["a"] = ga * a

            def pool_branch(cols=cols, held=held):
                p = jnp.dot(yp_ref[...], wp_ref[:, cols], preferred_element_type=jnp.float32)
                gp = jax.nn.sigmoid(gp_ref[:, cols].astype(jnp.float32) + gb_ref[1:2, cols])
                o_ref[:, cols] = (held["a"] + gp * p).astype(o_ref.dtype)

            thunks += [attn_branch, pool_branch]
        return thunks

    def run(ya_w, ya_r):
        products = merge_pieces(ya_r) if ya_r is not None else []
        stages = []
        if ya_w is not None:
            load_windows()
            stages = ([functools.partial(scores_softmax, i) for i in range(n_qb)]
                      + [functools.partial(attn_output, i, ya_w) for i in range(n_qb)])
        per_stage = -(-len(products) // max(len(stages), 1))
        for stage in stages:
            stage()
            for product in products[:per_stage]:
                product()
            products = products[per_stage:]
        for product in products:
            product()

    ya = (ya_even, ya_odd)

    @pl.when(t == 0)
    def _():
        run(ya[0], None)

    for parity in (0, 1):
        @pl.when((t > 0) & (t < n_row) & (t % 2 == parity))
        def _():
            run(ya[parity], ya[1 - parity])

    @pl.when(t == n_row)
    def _():
        run(None, ya[(n_row - 1) % 2])


def _band_bias(rel_bias):
    n_heads = rel_bias.shape[0]
    width = K_WIN + Q_BLK + 1
    far = rel_bias[:, 2 * MAX_REL:]
    n_far_left = K_BACK - MAX_REL
    vec = jnp.concatenate(
        [jnp.broadcast_to(far, (n_heads, n_far_left)),
         rel_bias[:, :0:-1],
         jnp.broadcast_to(far, (n_heads, width - n_far_left - 2 * MAX_REL))], axis=1)
    skew = jnp.broadcast_to(vec[:, None, :], (n_heads, Q_BLK, width))
    skew = skew.reshape(n_heads, Q_BLK * width)[:, :Q_BLK * (width - 1)]
    table = skew.reshape(n_heads, Q_BLK, width - 1)[:, :, :K_WIN].astype(jnp.float32) * LOG2E
    qc = jnp.arange(Q_BLK)[:, None] // CHUNK
    kc = jnp.arange(K_WIN)[None, :] // CHUNK
    valid = (kc >= qc) & (kc <= qc + N_LEFT_CHUNKS)
    return jnp.where(valid[None], table, NEG)


def _attn_merge(proj, bias, y_pool, wa, wp, gate_bias, seq, bn=1024):
    m = y_pool.shape[0]
    n_row = m // ROW_BLK
    n_col = D_MODEL // bn
    hw = HEADS_PER_STEP * HEAD_DIM
    n_qb = ROW_BLK // Q_BLK
    assert N_HEADS == n_col * HEADS_PER_STEP and seq % ROW_BLK == 0 and n_row % 2 == 0
    last = n_row - 1

    def att_col(t, j):
        return jnp.where(t > last, n_col - 1, j)

    def att_spec(off, prev=False):
        base = off // hw

        def index(t, j):
            rb = jnp.minimum(t, last)
            return (jnp.maximum(rb - 1, 0) if prev else rb, base + att_col(t, j))

        return pl.BlockSpec((ROW_BLK, hw), index)

    def mrg_rb(t):
        return jnp.maximum(t - 1, 0)

    def mrg_col(t, j):
        return jnp.where(t == 0, 0, j)

    ga_col = GA_OFF // bn
    gp_col = GP_OFF // bn
    in_specs = [
        att_spec(Q_OFF), att_spec(ZA_OFF),
        att_spec(K_OFF, prev=True), att_spec(K_OFF),
        att_spec(V_OFF, prev=True), att_spec(V_OFF),
        pl.BlockSpec((HEADS_PER_STEP, Q_BLK, K_WIN), lambda t, j: (att_col(t, j), 0, 0)),
        pl.BlockSpec((ROW_BLK, POOL_WIDTH), lambda t, j: (mrg_rb(t), 0)),
        pl.BlockSpec((ATTN_WIDTH, bn), lambda t, j: (0, mrg_col(t, j))),
        pl.BlockSpec((POOL_WIDTH, bn), lambda t, j: (0, mrg_col(t, j))),
        pl.BlockSpec((ROW_BLK, bn), lambda t, j: (mrg_rb(t), ga_col + mrg_col(t, j))),
        pl.BlockSpec((ROW_BLK, bn), lambda t, j: (mrg_rb(t), gp_col + mrg_col(t, j))),
        pl.BlockSpec((2, bn), lambda t, j: (0, mrg_col(t, j))),
    ]
    return pl.pallas_call(
        functools.partial(_attn_merge_kernel, n_row=n_row, blocks_per_seq=seq // ROW_BLK),
        grid=(n_row + 1, n_col),
        in_specs=in_specs,
        out_specs=pl.BlockSpec((ROW_BLK, bn), lambda t, j: (mrg_rb(t), mrg_col(t, j))),
        out_shape=jax.ShapeDtypeStruct((m, D_MODEL), jnp.bfloat16),
        scratch_shapes=[pltpu.VMEM((n_col, ROW_BLK, hw), jnp.bfloat16),
                        pltpu.VMEM((n_col, ROW_BLK, hw), jnp.bfloat16),
                        pltpu.VMEM((2 * ROW_BLK, hw), jnp.bfloat16),
                        pltpu.VMEM((2 * ROW_BLK, hw), jnp.bfloat16),
                        pltpu.VMEM((HEADS_PER_STEP, n_qb, Q_BLK, K_WIN), jnp.float32),
                        pltpu.VMEM((HEADS_PER_STEP, n_qb, Q_BLK, K_WIN), jnp.bfloat16),
                        pltpu.VMEM((HEADS_PER_STEP, n_qb, Q_BLK, 1), jnp.float32)],
        compiler_params=_params(("arbitrary", "arbitrary")),
        name="attn_merge",
    )(proj, proj, proj, proj, proj, proj, bias, y_pool, wa, wp, proj, proj, gate_bias)


def _out_kernel(m_ref, w_ref, x_ref, g_ref, o_hbm, r_scr, ssq_scr, scale_scr, sem, *, n_row, n_col):
    i = pl.program_id(0)
    j = pl.program_id(1)
    _, bm, bn = r_scr.shape
    d = n_col * bn

    def out_copy(row_blk, col_blk):
        return pltpu.make_async_copy(
            r_scr.at[col_blk],
            o_hbm.at[pl.ds(row_blk * bm, bm), pl.ds(col_blk * bn, bn)],
            sem.at[col_blk])

    @pl.when(i > 0)
    def _():
        out_copy(i - 1, j).wait()

    acc = jnp.dot(m_ref[...], w_ref[...], preferred_element_type=jnp.float32) + x_ref[...]
    r_scr[j] = acc
    sq = acc * acc
    part = sq[:, 0:LANES]
    for c in range(1, bn // LANES):
        part = part + sq[:, c * LANES:(c + 1) * LANES]

    @pl.when(j == 0)
    def _():
        ssq_scr[...] = part

    @pl.when(j > 0)
    def _():
        ssq_scr[...] += part

    @pl.when(j == n_col - 1)
    def _():
        ms = jnp.sum(ssq_scr[...], axis=-1, keepdims=True) * (1.0 / d)
        scale_scr[...] = jnp.broadcast_to(lax.rsqrt(ms + EPS), scale_scr.shape)
        for jj in range(n_col):
            def scale_rows(r, carry):
                rows = pl.ds(pl.multiple_of(r * NORM_ROWS, NORM_ROWS), NORM_ROWS)
                sc = scale_scr[rows, :]
                for c in range(bn // LANES):
                    cols = slice(c * LANES, (c + 1) * LANES)
                    gain = g_ref[:, jj * bn + c * LANES:jj * bn + (c + 1) * LANES]
                    r_scr[jj, rows, cols] = r_scr[jj, rows, cols] * sc * gain
                return carry

            lax.fori_loop(0, bm // NORM_ROWS, scale_rows, 0)
            out_copy(i, jj).start()

        @pl.when(i == n_row - 1)
        def _():
            for jj in range(n_col):
                out_copy(i, jj).wait()


def _out_proj_norm(m_arr, w_out, x2, gain, bm=1024, bn=512):
    m, d = x2.shape
    n_row, n_col = m // bm, d // bn
    return pl.pallas_call(
        functools.partial(_out_kernel, n_row=n_row, n_col=n_col),
        grid=(n_row, n_col),
        in_specs=[pl.BlockSpec((bm, d), lambda i, j: (i, 0)),
                  pl.BlockSpec((d, bn), lambda i, j: (0, j)),
                  pl.BlockSpec((bm, bn), lambda i, j: (i, j)),
                  pl.BlockSpec((1, d), lambda i, j: (0, 0))],
        out_specs=pl.BlockSpec(memory_space=pl.ANY),
        out_shape=jax.ShapeDtypeStruct((m, d), jnp.float32),
        scratch_shapes=[pltpu.VMEM((n_col, bm, bn), jnp.float32),
                        pltpu.VMEM((bm, LANES), jnp.float32),
                        pltpu.VMEM((bm, LANES), jnp.float32),
                        pltpu.SemaphoreType.DMA((n_col,))],
        compiler_params=_params(("arbitrary", "arbitrary")),
        name="out_proj_norm",
    )(m_arr, w_out, x2, gain.reshape(1, d))


def kernel(x, norm_gain, w_in, rel_bias, pool_w, pool_scale, w_out_attn, w_out_pool,
           gate_bias, w_out, final_gain):
    batch, seq, d = x.shape
    x2 = x.reshape(batch * seq, d)
    bf = jnp.bfloat16

    col_scale = jnp.where(jnp.arange(IN_COLS) < ATTN_WIDTH, SCORE_SCALE, 1.0).astype(jnp.float32)
    proj = _norm_in_proj(x2, norm_gain, w_in.astype(bf), col_scale)
    y_pool = _pooling(proj, pool_w, pool_scale, batch, seq)
    m = _attn_merge(proj, _band_bias(rel_bias), y_pool, w_out_attn.astype(bf),
                    w_out_pool.astype(bf), gate_bias, seq)
    out = _out_proj_norm(m, w_out.astype(bf), x2, final_gain)
    return out.reshape(batch, seq, d)
```

```python
import functools
import typing

import jax
import jax.numpy as jnp
from jax import lax
from jax.experimental import pallas as pl
from jax.experimental.pallas import tpu as pltpu

D_MODEL = 4096
CHUNK = 64
N_LEFT_CHUNKS = 8
ATTN_WIDTH = D_MODEL // 2
HEAD_DIM = 128
N_HEADS = ATTN_WIDTH // HEAD_DIM
MAX_REL = 128
POOL_WIDTH = D_MODEL // 2
POOL_WINDOWS = (2, 4, 8, 16)
POOL_GROUP_DIM = POOL_WIDTH // len(POOL_WINDOWS)
EPS = 1e-6

Q_OFF = 0
K_OFF = ATTN_WIDTH
V_OFF = 2 * ATTN_WIDTH
ZA_OFF = 3 * ATTN_WIDTH
U_OFF = 4 * ATTN_WIDTH
ZP_OFF = U_OFF + POOL_WIDTH
GA_OFF = ZP_OFF + POOL_WIDTH
GP_OFF = GA_OFF + D_MODEL
IN_COLS = GP_OFF + D_MODEL

Q_BLK = 2 * CHUNK
K_WIN = Q_BLK + N_LEFT_CHUNKS * CHUNK
K_BACK = N_LEFT_CHUNKS * CHUNK
HEADS_PER_STEP = 2
NEG = -1e30
LOG2E = 1.4426950408889634
SCORE_SCALE = HEAD_DIM ** -0.5 * LOG2E

POOL_TILE = 512
POOL_SUB = 128
NORM_ROWS = 128
LANES = 128
NORM_CHUNK = 128

HALF_COLS = IN_COLS // 2
assert U_OFF + POOL_WIDTH == HALF_COLS == ZP_OFF

VMEM_LIMIT = 56 * 1024 * 1024


def _params(sem):
    return pltpu.CompilerParams(dimension_semantics=sem, vmem_limit_bytes=VMEM_LIMIT)


class CastJob(typing.NamedTuple):
    src: jax.Array
    rows: int
    cols: int
    col_block: int


def _job_specs(jobs, n_col, n_steps):
    in_specs, out_specs, out_shapes, n_blocks = [], [], [], []
    for job in jobs:
        nb = job.src.shape[0] // job.rows
        assert job.src.shape[0] % job.rows == 0 and nb <= n_steps
        n_blocks.append(nb)
        in_specs.append(pl.BlockSpec(
            (job.rows, job.cols),
            lambda i, j, nb=nb, cb=job.col_block: (jnp.minimum(i * n_col + j, nb - 1), cb)))
        out_specs.append(pl.BlockSpec(
            (job.rows, job.cols), lambda i, j, nb=nb: (jnp.minimum(i * n_col + j, nb - 1), 0)))
        out_shapes.append(jax.ShapeDtypeStruct((job.src.shape[0], job.cols), jnp.bfloat16))
    return in_specs, out_specs, out_shapes, tuple(n_blocks)


def _run_cast_jobs(job_src, job_dst, job_blocks):
    step = pl.program_id(0) * pl.num_programs(1) + pl.program_id(1)
    for src, dst, n_blocks in zip(job_src, job_dst, job_blocks):
        @pl.when(step < n_blocks)
        def _():
            dst[...] = src[...].astype(dst.dtype)


def _norm_proj_kernel(*refs, n_row, job_blocks):
    n_jobs = len(job_blocks)
    x_ref, g_ref, w_ref, cs_ref = refs[:4]
    job_src = refs[4:4 + n_jobs]
    o_ref, hout_ref = refs[4 + n_jobs:6 + n_jobs]
    job_dst = refs[6 + n_jobs:6 + 2 * n_jobs]
    h_scr = refs[6 + 2 * n_jobs]
    i = pl.program_id(0)
    j = pl.program_id(1)
    n_chunks = h_scr.shape[1] // NORM_CHUNK

    def norm_chunk(slot):
        x = x_ref[...]
        ms = jnp.mean(x * x, axis=-1, keepdims=True)
        h = (x * lax.rsqrt(ms + EPS) * g_ref[...]).astype(h_scr.dtype)
        c = jnp.minimum(j, n_chunks - 1)
        h_scr[slot, pl.ds(pl.multiple_of(c * NORM_CHUNK, NORM_CHUNK), NORM_CHUNK), :] = h
        hout_ref[...] = h

    def matmul(slot):
        acc = jnp.dot(h_scr[slot], w_ref[...], preferred_element_type=jnp.float32)
        o_ref[...] = (acc * cs_ref[...]).astype(o_ref.dtype)

    @pl.when(i == 0)
    def _():
        norm_chunk(0)

    for parity in (0, 1):
        @pl.when((i > 0) & (i < n_row) & (i % 2 == parity))
        def _():
            matmul(1 - parity)
            norm_chunk(parity)

    @pl.when(i == n_row)
    def _():
        matmul((n_row - 1) % 2)

    _run_cast_jobs(job_src, job_dst, job_blocks)


def _norm_proj(x2, gain, w, col_scale, jobs, bm=1024, bn=1024):
    m, d = x2.shape
    _, n = w.shape
    n_row, n_col = m // bm, n // bn
    n_chunks = bm // NORM_CHUNK
    assert n_col >= n_chunks

    def x_map(i, j):
        chunk = jnp.where(i >= n_row, n_chunks - 1, jnp.minimum(j, n_chunks - 1))
        return (jnp.minimum(i, n_row - 1) * n_chunks + chunk, 0)

    def col(i, j):
        return jnp.where(i == 0, 0, j)

    job_in, job_out, job_shapes, job_blocks = _job_specs(jobs, n_col, (n_row + 1) * n_col)
    proj, h, *casts = pl.pallas_call(
        functools.partial(_norm_proj_kernel, n_row=n_row, job_blocks=job_blocks),
        grid=(n_row + 1, n_col),
        in_specs=[pl.BlockSpec((NORM_CHUNK, d), x_map),
                  pl.BlockSpec((1, d), lambda i, j: (0, 0)),
                  pl.BlockSpec((d, bn), lambda i, j: (0, col(i, j))),
                  pl.BlockSpec((1, bn), lambda i, j: (0, col(i, j)))] + job_in,
        out_specs=[pl.BlockSpec((bm, bn), lambda i, j: (jnp.maximum(i - 1, 0), col(i, j))),
                   pl.BlockSpec((NORM_CHUNK, d), x_map)] + job_out,
        out_shape=[jax.ShapeDtypeStruct((m, n), jnp.bfloat16),
                   jax.ShapeDtypeStruct((m, d), jnp.bfloat16)] + job_shapes,
        scratch_shapes=[pltpu.VMEM((2, bm, d), jnp.bfloat16)],
        compiler_params=_params(("arbitrary", "arbitrary")),
        name="norm_proj",
    )(x2, gain.reshape(1, d), w, col_scale.reshape(1, n), *[job.src for job in jobs])
    return proj, h, casts


def _proj_kernel(*refs, job_blocks):
    n_jobs = len(job_blocks)
    h_ref, w_ref, cs_ref = refs[:3]
    job_src = refs[3:3 + n_jobs]
    o_ref = refs[3 + n_jobs]
    job_dst = refs[4 + n_jobs:4 + 2 * n_jobs]
    acc = jnp.dot(h_ref[...], w_ref[...], preferred_element_type=jnp.float32)
    o_ref[...] = (acc * cs_ref[...]).astype(o_ref.dtype)
    _run_cast_jobs(job_src, job_dst, job_blocks)


def _proj(h, w, col_scale, jobs, bm=1024, bn=1024):
    m, d = h.shape
    _, n = w.shape
    n_row, n_col = m // bm, n // bn
    job_in, job_out, job_shapes, job_blocks = _job_specs(jobs, n_col, n_row * n_col)
    proj, *casts = pl.pallas_call(
        functools.partial(_proj_kernel, job_blocks=job_blocks),
        grid=(n_row, n_col),
        in_specs=[pl.BlockSpec((bm, d), lambda i, j: (i, 0)),
                  pl.BlockSpec((d, bn), lambda i, j: (0, j)),
                  pl.BlockSpec((1, bn), lambda i, j: (0, j))] + job_in,
        out_specs=[pl.BlockSpec((bm, bn), lambda i, j: (i, j))] + job_out,
        out_shape=[jax.ShapeDtypeStruct((m, n), jnp.bfloat16)] + job_shapes,
        compiler_params=_params(("arbitrary", "arbitrary")),
        name="proj",
    )(h, w, col_scale.reshape(1, n), *[job.src for job in jobs])
    return proj, casts


def _silu(z):
    return z * jax.nn.sigmoid(z)


def _pool_kernel(u_ref, halo_ref, z_ref, a_ref, pw_ref, ps_ref, o_ref, buf_ref):
    s = pl.program_id(1)
    halo = halo_ref[...]
    halo = jnp.where(s == 0, jnp.zeros_like(halo), halo)
    buf_ref[0:POOL_SUB, :] = halo
    buf_ref[POOL_SUB:, :] = u_ref[...]
    t0 = s * POOL_TILE
    row = lax.broadcasted_iota(jnp.int32, (POOL_SUB, 1), 0)
    n_sub = POOL_TILE // POOL_SUB

    for g, w in enumerate(POOL_WINDOWS):
        cs = slice(g * POOL_GROUP_DIM, (g + 1) * POOL_GROUP_DIM)
        ds = []
        for j in range(n_sub):
            win = buf_ref[j * POOL_SUB:(j + 2) * POOL_SUB, cs]
            wsum = jnp.dot(a_ref[g], win, preferred_element_type=jnp.float32)
            t = t0 + j * POOL_SUB + row
            cnt = jnp.minimum(t + 1, w).astype(jnp.float32)
            cur = buf_ref[(j + 1) * POOL_SUB:(j + 2) * POOL_SUB, cs].astype(jnp.float32)
            ds.append((wsum / cnt - cur).astype(jnp.bfloat16))
        d = jnp.concatenate(ds, axis=0)
        y = jnp.dot(d, pw_ref[g], preferred_element_type=jnp.float32)
        z = z_ref[:, cs].astype(jnp.float32)
        o_ref[:, cs] = (y * ps_ref[:, cs] * _silu(z)).astype(o_ref.dtype)


def _window_matrices():
    r = jnp.arange(POOL_SUB)[:, None] + POOL_SUB
    c = jnp.arange(2 * POOL_SUB)[None, :]
    mats = [((c <= r) & (c > r - w)) for w in POOL_WINDOWS]
    return jnp.stack(mats).astype(jnp.bfloat16)


def _pooling(proj_lo, proj_hi, pool_w, pool_scale, batch, seq):
    n_tiles = seq // POOL_TILE
    sub_per_tile = POOL_TILE // POOL_SUB
    sub_per_seq = seq // POOL_SUB
    u_col = U_OFF // POOL_WIDTH
    z_col = (ZP_OFF - HALF_COLS) // POOL_WIDTH
    g = len(POOL_WINDOWS)

    def halo_map(b, s):
        return (jnp.maximum(b * sub_per_seq + s * sub_per_tile - 1, 0), u_col)

    return pl.pallas_call(
        _pool_kernel,
        grid=(batch, n_tiles),
        in_specs=[pl.BlockSpec((POOL_TILE, POOL_WIDTH), lambda b, s: (b * n_tiles + s, u_col)),
                  pl.BlockSpec((POOL_SUB, POOL_WIDTH), halo_map),
                  pl.BlockSpec((POOL_TILE, POOL_WIDTH), lambda b, s: (b * n_tiles + s, z_col)),
                  pl.BlockSpec((g, POOL_SUB, 2 * POOL_SUB), lambda b, s: (0, 0, 0)),
                  pl.BlockSpec((g, POOL_GROUP_DIM, POOL_GROUP_DIM), lambda b, s: (0, 0, 0)),
                  pl.BlockSpec((1, POOL_WIDTH), lambda b, s: (0, 0))],
        out_specs=pl.BlockSpec((POOL_TILE, POOL_WIDTH), lambda b, s: (b * n_tiles + s, 0)),
        out_shape=jax.ShapeDtypeStruct((batch * seq, POOL_WIDTH), jnp.bfloat16),
        scratch_shapes=[pltpu.VMEM((POOL_TILE + POOL_SUB, POOL_WIDTH), jnp.bfloat16)],
        compiler_params=_params(("parallel", "arbitrary")),
        name="ms_pool",
    )(proj_lo, proj_lo, proj_hi, _window_matrices(), pool_w.astype(jnp.bfloat16),
      pool_scale.reshape(1, POOL_WIDTH))


def _attn_kernel(q_ref, k_ref, v_ref, z_ref, bias_ref, o_ref, s_scr, p_scr, l_scr, *, seq):
    n_blk = seq // Q_BLK
    n_edge = K_BACK // Q_BLK
    heads = range(HEADS_PER_STEP)

    def head_cols(hh):
        return slice(hh * HEAD_DIM, (hh + 1) * HEAD_DIM)

    def rows(j):
        start = j * Q_BLK
        return pl.ds(start if isinstance(j, int) else pl.multiple_of(start, Q_BLK), Q_BLK)

    def key_rows(j):
        if isinstance(j, int):
            return pl.ds(max(j * Q_BLK - K_BACK, 0), K_WIN)
        return pl.ds(pl.multiple_of(j * Q_BLK - K_BACK, Q_BLK), K_WIN)

    def block_bias(hh, j):
        if isinstance(j, int) and j < n_edge:
            shift = K_BACK - j * Q_BLK
            pad = jnp.full((Q_BLK, shift), NEG, jnp.float32)
            return jnp.concatenate([bias_ref[hh, :, shift:], pad], axis=1)
        return bias_ref[hh]

    def scores(j, slot):
        for hh in heads:
            cs = head_cols(hh)
            s = lax.dot_general(q_ref[rows(j), cs], k_ref[key_rows(j), cs],
                                (((1,), (1,)), ((), ())), preferred_element_type=jnp.float32)
            s_scr[hh, slot] = s + block_bias(hh, j)

    def softmax(j, slot):
        for hh in heads:
            s = s_scr[hh, slot]
            p = jnp.exp2(s - jnp.max(s, axis=-1, keepdims=True))
            l_scr[hh, slot] = jnp.sum(p, axis=-1, keepdims=True)
            p_scr[hh, slot] = p.astype(p_scr.dtype)

    def output(j, slot):
        for hh in heads:
            cs = head_cols(hh)
            o = jnp.dot(p_scr[hh, slot], v_ref[key_rows(j), cs],
                        preferred_element_type=jnp.float32)
            z = z_ref[rows(j), cs].astype(jnp.float32)
            y = o * z / ((1.0 + jnp.exp(-z)) * l_scr[hh, slot])
            o_ref[rows(j), cs] = y.astype(o_ref.dtype)

    def step(t, parity, first=0, last=n_blk - 1):
        static = isinstance(t, int)
        if not static or first <= t - 2 <= last:
            output(t - 2, parity)
        if not static or first <= t - 1 <= last:
            softmax(t - 1, 1 - parity)
        if not static or first <= t <= last:
            scores(t, parity)

    n_head_steps = n_edge + 2
    assert n_head_steps % 2 == 0 and (n_blk - n_head_steps) % 2 == 0
    for t in range(n_head_steps):
        step(t, t % 2)

    def body(i, carry):
        t = n_head_steps + 2 * i
        step(t, 0)
        step(t + 1, 1)
        return carry

    lax.fori_loop(0, (n_blk - n_head_steps) // 2, body, 0)
    for t in range(n_blk, n_blk + 2):
        step(t, t % 2)


def _band_bias(rel_bias):
    n_heads = rel_bias.shape[0]
    width = K_WIN + Q_BLK + 1
    far = rel_bias[:, 2 * MAX_REL:]
    n_far_left = K_BACK - MAX_REL
    vec = jnp.concatenate(
        [jnp.broadcast_to(far, (n_heads, n_far_left)),
         rel_bias[:, :0:-1],
         jnp.broadcast_to(far, (n_heads, width - n_far_left - 2 * MAX_REL))], axis=1)
    skew = jnp.broadcast_to(vec[:, None, :], (n_heads, Q_BLK, width))
    skew = skew.reshape(n_heads, Q_BLK * width)[:, :Q_BLK * (width - 1)]
    table = skew.reshape(n_heads, Q_BLK, width - 1)[:, :, :K_WIN].astype(jnp.float32) * LOG2E
    qc = jnp.arange(Q_BLK)[:, None] // CHUNK
    kc = jnp.arange(K_WIN)[None, :] // CHUNK
    valid = (kc >= qc) & (kc <= qc + N_LEFT_CHUNKS)
    return jnp.where(valid[None], table, NEG)


def _attention(proj, bias, batch, seq):
    hw = HEADS_PER_STEP * HEAD_DIM
    n_hg = N_HEADS // HEADS_PER_STEP

    def col_spec(off):
        base = off // hw
        return pl.BlockSpec((seq, hw), lambda b, h: (b, base + h))

    return pl.pallas_call(
        functools.partial(_attn_kernel, seq=seq),
        grid=(batch, n_hg),
        in_specs=[col_spec(Q_OFF), col_spec(K_OFF), col_spec(V_OFF), col_spec(ZA_OFF),
                  pl.BlockSpec((HEADS_PER_STEP, Q_BLK, K_WIN), lambda b, h: (h, 0, 0))],
        out_specs=pl.BlockSpec((seq, hw), lambda b, h: (b, h)),
        out_shape=jax.ShapeDtypeStruct((batch * seq, ATTN_WIDTH), jnp.bfloat16),
        scratch_shapes=[pltpu.VMEM((HEADS_PER_STEP, 2, Q_BLK, K_WIN), jnp.float32),
                        pltpu.VMEM((HEADS_PER_STEP, 2, Q_BLK, K_WIN), jnp.bfloat16),
                        pltpu.VMEM((HEADS_PER_STEP, 2, Q_BLK, 1), jnp.float32)],
        compiler_params=_params(("parallel", "arbitrary")),
        name="band_attn",
    )(proj, proj, proj, proj, bias)


def _merge_kernel(ya_ref, yp_ref, wa_ref, wp_ref, ga_ref, gp_ref, gb_ref, o_ref):
    a = jnp.dot(ya_ref[...], wa_ref[...], preferred_element_type=jnp.float32)
    p = jnp.dot(yp_ref[...], wp_ref[...], preferred_element_type=jnp.float32)
    ga = jax.nn.sigmoid(ga_ref[...].astype(jnp.float32) + gb_ref[0:1, :])
    gp = jax.nn.sigmoid(gp_ref[...].astype(jnp.float32) + gb_ref[1:2, :])
    o_ref[...] = (ga * a + gp * p).astype(o_ref.dtype)


def _merge(y_attn, y_pool, wa, wp, proj_hi, gate_bias, bm=1024, bn=1024):
    m = y_attn.shape[0]
    ga_col = (GA_OFF - HALF_COLS) // bn
    gp_col = (GP_OFF - HALF_COLS) // bn
    return pl.pallas_call(
        _merge_kernel,
        grid=(m // bm, D_MODEL // bn),
        in_specs=[pl.BlockSpec((bm, ATTN_WIDTH), lambda i, j: (i, 0)),
                  pl.BlockSpec((bm, POOL_WIDTH), lambda i, j: (i, 0)),
                  pl.BlockSpec((ATTN_WIDTH, bn), lambda i, j: (0, j)),
                  pl.BlockSpec((POOL_WIDTH, bn), lambda i, j: (0, j)),
                  pl.BlockSpec((bm, bn), lambda i, j: (i, ga_col + j)),
                  pl.BlockSpec((bm, bn), lambda i, j: (i, gp_col + j)),
                  pl.BlockSpec((2, bn), lambda i, j: (0, j))],
        out_specs=pl.BlockSpec((bm, bn), lambda i, j: (i, j)),
        out_shape=jax.ShapeDtypeStruct((m, D_MODEL), jnp.bfloat16),
        compiler_params=_params(("parallel", "arbitrary")),
        name="gated_merge",
    )(y_attn, y_pool, wa, wp, proj_hi, proj_hi, gate_bias)


def _out_kernel(m_ref, w_ref, x_ref, g_ref, o_hbm, r_scr, ssq_scr, scale_scr, sem, *, n_row, n_col):
    i = pl.program_id(0)
    j = pl.program_id(1)
    _, bm, bn = r_scr.shape
    d = n_col * bn

    def out_copy(row_blk, col_blk):
        return pltpu.make_async_copy(
            r_scr.at[col_blk],
            o_hbm.at[pl.ds(row_blk * bm, bm), pl.ds(col_blk * bn, bn)],
            sem.at[col_blk])

    @pl.when(i > 0)
    def _():
        out_copy(i - 1, j).wait()

    acc = jnp.dot(m_ref[...], w_ref[...], preferred_element_type=jnp.float32) + x_ref[...]
    r_scr[j] = acc
    sq = acc * acc
    part = sq[:, 0:LANES]
    for c in range(1, bn // LANES):
        part = part + sq[:, c * LANES:(c + 1) * LANES]

    @pl.when(j == 0)
    def _():
        ssq_scr[...] = part

    @pl.when(j > 0)
    def _():
        ssq_scr[...] += part

    @pl.when(j == n_col - 1)
    def _():
        ms = jnp.sum(ssq_scr[...], axis=-1, keepdims=True) * (1.0 / d)
        scale_scr[...] = jnp.broadcast_to(lax.rsqrt(ms + EPS), scale_scr.shape)
        for jj in range(n_col):
            def scale_rows(r, carry):
                rows = pl.ds(pl.multiple_of(r * NORM_ROWS, NORM_ROWS), NORM_ROWS)
                sc = scale_scr[rows, :]
                for c in range(bn // LANES):
                    cols = slice(c * LANES, (c + 1) * LANES)
                    gain = g_ref[:, jj * bn + c * LANES:jj * bn + (c + 1) * LANES]
                    r_scr[jj, rows, cols] = r_scr[jj, rows, cols] * sc * gain
                return carry

            lax.fori_loop(0, bm // NORM_ROWS, scale_rows, 0)
            out_copy(i, jj).start()

        @pl.when(i == n_row - 1)
        def _():
            for jj in range(n_col):
                out_copy(i, jj).wait()


def _out_proj_norm(m_arr, w_out, x2, gain, bm=1024, bn=512):
    m, d = x2.shape
    n_row, n_col = m // bm, d // bn
    return pl.pallas_call(
        functools.partial(_out_kernel, n_row=n_row, n_col=n_col),
        grid=(n_row, n_col),
        in_specs=[pl.BlockSpec((bm, d), lambda i, j: (i, 0)),
                  pl.BlockSpec((d, bn), lambda i, j: (0, j)),
                  pl.BlockSpec((bm, bn), lambda i, j: (i, j)),
                  pl.BlockSpec((1, d), lambda i, j: (0, 0))],
        out_specs=pl.BlockSpec(memory_space=pl.ANY),
        out_shape=jax.ShapeDtypeStruct((m, d), jnp.float32),
        scratch_shapes=[pltpu.VMEM((n_col, bm, bn), jnp.float32),
                        pltpu.VMEM((bm, LANES), jnp.float32),
                        pltpu.VMEM((bm, LANES), jnp.float32),
                        pltpu.SemaphoreType.DMA((n_col,))],
        compiler_params=_params(("arbitrary", "arbitrary")),
        name="out_proj_norm",
    )(m_arr, w_out, x2, gain.reshape(1, d))


def kernel(x, norm_gain, w_in, rel_bias, pool_w, pool_scale, w_out_attn, w_out_pool,
           gate_bias, w_out, final_gain):
    batch, seq, d = x.shape
    x2 = x.reshape(batch * seq, d)
    bf = jnp.bfloat16

    col_scale = jnp.where(jnp.arange(IN_COLS) < ATTN_WIDTH, SCORE_SCALE, 1.0).astype(jnp.float32)
    w_hi = w_in[:, HALF_COLS:].astype(bf)
    proj_hi, h, (w_lo,) = _norm_proj(
        x2, norm_gain, w_hi, col_scale[HALF_COLS:], [CastJob(w_in, 32, HALF_COLS, 0)])
    proj_lo, (wa, wp, wo) = _proj(
        h, w_lo, col_scale[:HALF_COLS],
        [CastJob(w_out_attn, 64, D_MODEL, 0), CastJob(w_out_pool, 64, D_MODEL, 0),
         CastJob(w_out, 64, D_MODEL, 0)])
    y_pool = _pooling(proj_lo, proj_hi, pool_w, pool_scale, batch, seq)
    y_attn = _attention(proj_lo, _band_bias(rel_bias), batch, seq)
    m = _merge(y_attn, y_pool, wa, wp, proj_hi, gate_bias)
    out = _out_proj_norm(m, wo, x2, final_gain)
    return out.reshape(batch, seq, d)
```

```python
import functools
import typing

import jax
import jax.numpy as jnp
from jax import lax
from jax.experimental import pallas as pl
from jax.experimental.pallas import tpu as pltpu

D_MODEL = 4096
CHUNK = 64
N_LEFT_CHUNKS = 8
ATTN_WIDTH = D_MODEL // 2
HEAD_DIM = 128
N_HEADS = ATTN_WIDTH // HEAD_DIM
MAX_REL = 128
POOL_WIDTH = D_MODEL // 2
POOL_WINDOWS = (2, 4, 8, 16)
POOL_GROUP_DIM = POOL_WIDTH // len(POOL_WINDOWS)
EPS = 1e-6

Q_OFF = 0
K_OFF = ATTN_WIDTH
V_OFF = 2 * ATTN_WIDTH
ZA_OFF = 3 * ATTN_WIDTH
U_OFF = 4 * ATTN_WIDTH
ZP_OFF = U_OFF + POOL_WIDTH
GA_OFF = ZP_OFF + POOL_WIDTH
GP_OFF = GA_OFF + D_MODEL
IN_COLS = GP_OFF + D_MODEL

Q_BLK = 2 * CHUNK
K_WIN = Q_BLK + N_LEFT_CHUNKS * CHUNK
K_BACK = N_LEFT_CHUNKS * CHUNK
HEADS_PER_STEP = 4
NEG = -1e30
LOG2E = 1.4426950408889634
SCORE_SCALE = HEAD_DIM ** -0.5 * LOG2E

POOL_TILE = 512
POOL_SUB = 128
NORM_ROWS = 128
LANES = 128
SLAB = 512
NORM_CHUNK = 128

HALF_COLS = IN_COLS // 2
assert U_OFF + POOL_WIDTH == HALF_COLS == ZP_OFF

VMEM_LIMIT = 56 * 1024 * 1024


def _params(sem):
    return pltpu.CompilerParams(dimension_semantics=sem, vmem_limit_bytes=VMEM_LIMIT)


class CastJob(typing.NamedTuple):
    src: jax.Array
    rows: int
    cols: int
    col_block: int


def _job_specs(jobs, n_col, n_steps):
    in_specs, out_specs, out_shapes, n_blocks = [], [], [], []
    for job in jobs:
        nb = job.src.shape[0] // job.rows
        assert job.src.shape[0] % job.rows == 0 and nb <= n_steps
        n_blocks.append(nb)
        in_specs.append(pl.BlockSpec(
            (job.rows, job.cols),
            lambda i, j, nb=nb, cb=job.col_block: (jnp.minimum(i * n_col + j, nb - 1), cb)))
        out_specs.append(pl.BlockSpec(
            (job.rows, job.cols), lambda i, j, nb=nb: (jnp.minimum(i * n_col + j, nb - 1), 0)))
        out_shapes.append(jax.ShapeDtypeStruct((job.src.shape[0], job.cols), jnp.bfloat16))
    return in_specs, out_specs, out_shapes, tuple(n_blocks)


def _run_cast_jobs(job_src, job_dst, job_blocks):
    step = pl.program_id(0) * pl.num_programs(1) + pl.program_id(1)
    for src, dst, n_blocks in zip(job_src, job_dst, job_blocks):
        @pl.when(step < n_blocks)
        def _():
            dst[...] = src[...].astype(dst.dtype)


def _norm_proj_kernel(*refs, n_row, job_blocks):
    n_jobs = len(job_blocks)
    x_ref, g_ref, w_ref, cs_ref = refs[:4]
    job_src = refs[4:4 + n_jobs]
    o_ref, hout_ref = refs[4 + n_jobs:6 + n_jobs]
    job_dst = refs[6 + n_jobs:6 + 2 * n_jobs]
    h_scr = refs[6 + 2 * n_jobs]
    i = pl.program_id(0)
    j = pl.program_id(1)
    n_chunks = h_scr.shape[1] // NORM_CHUNK

    def norm_chunk(slot):
        x = x_ref[...]
        ms = jnp.mean(x * x, axis=-1, keepdims=True)
        h = (x * lax.rsqrt(ms + EPS) * g_ref[...]).astype(h_scr.dtype)
        c = jnp.minimum(j, n_chunks - 1)
        h_scr[slot, pl.ds(pl.multiple_of(c * NORM_CHUNK, NORM_CHUNK), NORM_CHUNK), :] = h
        hout_ref[...] = h

    def matmul(slot):
        acc = jnp.dot(h_scr[slot], w_ref[...], preferred_element_type=jnp.float32)
        o_ref[...] = (acc * cs_ref[...]).astype(o_ref.dtype)

    @pl.when(i == 0)
    def _():
        norm_chunk(0)

    for parity in (0, 1):
        @pl.when((i > 0) & (i < n_row) & (i % 2 == parity))
        def _():
            matmul(1 - parity)
            norm_chunk(parity)

    @pl.when(i == n_row)
    def _():
        matmul((n_row - 1) % 2)

    _run_cast_jobs(job_src, job_dst, job_blocks)


def _norm_proj(x2, gain, w, col_scale, jobs, bm=1024, bn=1024):
    m, d = x2.shape
    _, n = w.shape
    n_row, n_col = m // bm, n // bn
    n_chunks = bm // NORM_CHUNK
    assert n_col >= n_chunks

    def x_map(i, j):
        chunk = jnp.where(i >= n_row, n_chunks - 1, jnp.minimum(j, n_chunks - 1))
        return (jnp.minimum(i, n_row - 1) * n_chunks + chunk, 0)

    def col(i, j):
        return jnp.where(i == 0, 0, j)

    job_in, job_out, job_shapes, job_blocks = _job_specs(jobs, n_col, (n_row + 1) * n_col)
    proj, h, *casts = pl.pallas_call(
        functools.partial(_norm_proj_kernel, n_row=n_row, job_blocks=job_blocks),
        grid=(n_row + 1, n_col),
        in_specs=[pl.BlockSpec((NORM_CHUNK, d), x_map),
                  pl.BlockSpec((1, d), lambda i, j: (0, 0)),
                  pl.BlockSpec((d, bn), lambda i, j: (0, col(i, j))),
                  pl.BlockSpec((1, bn), lambda i, j: (0, col(i, j)))] + job_in,
        out_specs=[pl.BlockSpec((bm, bn), lambda i, j: (jnp.maximum(i - 1, 0), col(i, j))),
                   pl.BlockSpec((NORM_CHUNK, d), x_map)] + job_out,
        out_shape=[jax.ShapeDtypeStruct((m, n), jnp.bfloat16),
                   jax.ShapeDtypeStruct((m, d), jnp.bfloat16)] + job_shapes,
        scratch_shapes=[pltpu.VMEM((2, bm, d), jnp.bfloat16)],
        compiler_params=_params(("arbitrary", "arbitrary")),
        name="norm_proj",
    )(x2, gain.reshape(1, d), w, col_scale.reshape(1, n), *[job.src for job in jobs])
    return proj, h, casts


def _proj_kernel(*refs, job_blocks):
    n_jobs = len(job_blocks)
    h_ref, w_ref, cs_ref = refs[:3]
    job_src = refs[3:3 + n_jobs]
    o_ref = refs[3 + n_jobs]
    job_dst = refs[4 + n_jobs:4 + 2 * n_jobs]
    acc = jnp.dot(h_ref[...], w_ref[...], preferred_element_type=jnp.float32)
    res = (acc * cs_ref[...]).astype(o_ref.dtype)
    for c in range(o_ref.shape[0]):
        o_ref[c] = res[:, c * SLAB:(c + 1) * SLAB]
    _run_cast_jobs(job_src, job_dst, job_blocks)


def _proj(h, w, col_scale, jobs, bm=1024, bn=1024):
    m, d = h.shape
    _, n = w.shape
    n_row, n_col = m // bm, n // bn
    job_in, job_out, job_shapes, job_blocks = _job_specs(jobs, n_col, n_row * n_col)
    proj, *casts = pl.pallas_call(
        functools.partial(_proj_kernel, job_blocks=job_blocks),
        grid=(n_row, n_col),
        in_specs=[pl.BlockSpec((bm, d), lambda i, j: (i, 0)),
                  pl.BlockSpec((d, bn), lambda i, j: (0, j)),
                  pl.BlockSpec((1, bn), lambda i, j: (0, j))] + job_in,
        out_specs=[pl.BlockSpec((bn // SLAB, bm, SLAB), lambda i, j: (j, i, 0))] + job_out,
        out_shape=[jax.ShapeDtypeStruct((n // SLAB, m, SLAB), jnp.bfloat16)] + job_shapes,
        compiler_params=_params(("arbitrary", "arbitrary")),
        name="proj",
    )(h, w, col_scale.reshape(1, n), *[job.src for job in jobs])
    return proj, casts


def _silu(z):
    return z * jax.nn.sigmoid(z)


def _pool_kernel(u_ref, halo_ref, z_ref, a_ref, pw_ref, ps_ref, o_ref, buf_ref):
    s = pl.program_id(1)
    halo = halo_ref[...]
    halo = jnp.where(s == 0, jnp.zeros_like(halo), halo)
    buf_ref[:, 0:POOL_SUB, :] = halo
    buf_ref[:, POOL_SUB:, :] = u_ref[...]
    t0 = s * POOL_TILE
    row = lax.broadcasted_iota(jnp.int32, (POOL_SUB, 1), 0)
    n_sub = POOL_TILE // POOL_SUB

    for g, w in enumerate(POOL_WINDOWS):
        cs = slice(g * POOL_GROUP_DIM, (g + 1) * POOL_GROUP_DIM)
        ds = []
        for j in range(n_sub):
            win = buf_ref[g, j * POOL_SUB:(j + 2) * POOL_SUB, :]
            wsum = jnp.dot(a_ref[g], win, preferred_element_type=jnp.float32)
            t = t0 + j * POOL_SUB + row
            cnt = jnp.minimum(t + 1, w).astype(jnp.float32)
            cur = buf_ref[g, (j + 1) * POOL_SUB:(j + 2) * POOL_SUB, :].astype(jnp.float32)
            ds.append((wsum / cnt - cur).astype(jnp.bfloat16))
        d = jnp.concatenate(ds, axis=0)
        y = jnp.dot(d, pw_ref[g], preferred_element_type=jnp.float32)
        z = z_ref[:, cs].astype(jnp.float32)
        o_ref[:, cs] = (y * ps_ref[:, cs] * _silu(z)).astype(o_ref.dtype)


def _window_matrices():
    r = jnp.arange(POOL_SUB)[:, None] + POOL_SUB
    c = jnp.arange(2 * POOL_SUB)[None, :]
    mats = [((c <= r) & (c > r - w)) for w in POOL_WINDOWS]
    return jnp.stack(mats).astype(jnp.bfloat16)


def _pooling(proj_lo, proj_hi, pool_w, pool_scale, batch, seq):
    n_tiles = seq // POOL_TILE
    sub_per_tile = POOL_TILE // POOL_SUB
    sub_per_seq = seq // POOL_SUB
    g = len(POOL_WINDOWS)
    assert POOL_GROUP_DIM == SLAB and U_OFF % POOL_WIDTH == 0
    u_slabs = U_OFF // POOL_WIDTH
    z_col = (ZP_OFF - HALF_COLS) // POOL_WIDTH

    def halo_map(b, s):
        return (u_slabs, jnp.maximum(b * sub_per_seq + s * sub_per_tile - 1, 0), 0)

    return pl.pallas_call(
        _pool_kernel,
        grid=(batch, n_tiles),
        in_specs=[pl.BlockSpec((g, POOL_TILE, SLAB), lambda b, s: (u_slabs, b * n_tiles + s, 0)),
                  pl.BlockSpec((g, POOL_SUB, SLAB), halo_map),
                  pl.BlockSpec((POOL_TILE, POOL_WIDTH), lambda b, s: (b * n_tiles + s, z_col)),
                  pl.BlockSpec((g, POOL_SUB, 2 * POOL_SUB), lambda b, s: (0, 0, 0)),
                  pl.BlockSpec((g, POOL_GROUP_DIM, POOL_GROUP_DIM), lambda b, s: (0, 0, 0)),
                  pl.BlockSpec((1, POOL_WIDTH), lambda b, s: (0, 0))],
        out_specs=pl.BlockSpec((POOL_TILE, POOL_WIDTH), lambda b, s: (b * n_tiles + s, 0)),
        out_shape=jax.ShapeDtypeStruct((batch * seq, POOL_WIDTH), jnp.bfloat16),
        scratch_shapes=[pltpu.VMEM((g, POOL_TILE + POOL_SUB, SLAB), jnp.bfloat16)],
        compiler_params=_params(("parallel", "arbitrary")),
        name="ms_pool",
    )(proj_lo, proj_lo, proj_hi, _window_matrices(), pool_w.astype(jnp.bfloat16),
      pool_scale.reshape(1, POOL_WIDTH))


def _attn_kernel(q_ref, k_ref, v_ref, z_ref, bias_ref, o_ref, s_scr, p_scr, l_scr, *, seq):
    n_blk = seq // Q_BLK
    n_edge = K_BACK // Q_BLK
    heads = range(HEADS_PER_STEP)

    def head_cols(hh):
        return slice(hh * HEAD_DIM, (hh + 1) * HEAD_DIM)

    def rows(j):
        start = j * Q_BLK
        return pl.ds(start if isinstance(j, int) else pl.multiple_of(start, Q_BLK), Q_BLK)

    def key_rows(j):
        if isinstance(j, int):
            return pl.ds(max(j * Q_BLK - K_BACK, 0), K_WIN)
        return pl.ds(pl.multiple_of(j * Q_BLK - K_BACK, Q_BLK), K_WIN)

    def block_bias(hh, j):
        if isinstance(j, int) and j < n_edge:
            shift = K_BACK - j * Q_BLK
            pad = jnp.full((Q_BLK, shift), NEG, jnp.float32)
            return jnp.concatenate([bias_ref[hh, :, shift:], pad], axis=1)
        return bias_ref[hh]

    def scores(j, slot):
        for hh in heads:
            cs = head_cols(hh)
            s = lax.dot_general(q_ref[rows(j), cs], k_ref[key_rows(j), cs],
                                (((1,), (1,)), ((), ())), preferred_element_type=jnp.float32)
            s_scr[hh, slot] = s + block_bias(hh, j)

    def softmax(j, slot):
        for hh in heads:
            s = s_scr[hh, slot]
            p = jnp.exp2(s - jnp.max(s, axis=-1, keepdims=True))
            l_scr[hh, slot] = jnp.sum(p, axis=-1, keepdims=True)
            p_scr[hh, slot] = p.astype(p_scr.dtype)

    def output(j, slot):
        for hh in heads:
            cs = head_cols(hh)
            o = jnp.dot(p_scr[hh, slot], v_ref[key_rows(j), cs],
                        preferred_element_type=jnp.float32)
            z = z_ref[rows(j), cs].astype(jnp.float32)
            y = o * z / ((1.0 + jnp.exp(-z)) * l_scr[hh, slot])
            o_ref[rows(j), cs] = y.astype(o_ref.dtype)

    def step(t, parity, first=0, last=n_blk - 1):
        static = isinstance(t, int)
        if not static or first <= t - 2 <= last:
            output(t - 2, parity)
        if not static or first <= t - 1 <= last:
            softmax(t - 1, 1 - parity)
        if not static or first <= t <= last:
            scores(t, parity)

    n_head_steps = n_edge + 2
    assert n_head_steps % 2 == 0 and (n_blk - n_head_steps) % 2 == 0
    for t in range(n_head_steps):
        step(t, t % 2)

    def body(i, carry):
        t = n_head_steps + 2 * i
        step(t, 0)
        step(t + 1, 1)
        return carry

    lax.fori_loop(0, (n_blk - n_head_steps) // 2, body, 0)
    for t in range(n_blk, n_blk + 2):
        step(t, t % 2)


def _band_bias(rel_bias):
    n_heads = rel_bias.shape[0]
    width = K_WIN + Q_BLK + 1
    far = rel_bias[:, 2 * MAX_REL:]
    n_far_left = K_BACK - MAX_REL
    vec = jnp.concatenate(
        [jnp.broadcast_to(far, (n_heads, n_far_left)),
         rel_bias[:, :0:-1],
         jnp.broadcast_to(far, (n_heads, width - n_far_left - 2 * MAX_REL))], axis=1)
    skew = jnp.broadcast_to(vec[:, None, :], (n_heads, Q_BLK, width))
    skew = skew.reshape(n_heads, Q_BLK * width)[:, :Q_BLK * (width - 1)]
    table = skew.reshape(n_heads, Q_BLK, width - 1)[:, :, :K_WIN].astype(jnp.float32) * LOG2E
    qc = jnp.arange(Q_BLK)[:, None] // CHUNK
    kc = jnp.arange(K_WIN)[None, :] // CHUNK
    valid = (kc >= qc) & (kc <= qc + N_LEFT_CHUNKS)
    return jnp.where(valid[None], table, NEG)


def _attention(proj, bias, batch, seq):
    hw = HEADS_PER_STEP * HEAD_DIM
    n_hg = N_HEADS // HEADS_PER_STEP
    assert hw == SLAB

    def col_spec(off):
        base = off // SLAB
        return pl.BlockSpec((None, seq, SLAB), lambda b, h: (base + h, b, 0))

    return pl.pallas_call(
        functools.partial(_attn_kernel, seq=seq),
        grid=(batch, n_hg),
        in_specs=[col_spec(Q_OFF), col_spec(K_OFF), col_spec(V_OFF), col_spec(ZA_OFF),
                  pl.BlockSpec((HEADS_PER_STEP, Q_BLK, K_WIN), lambda b, h: (h, 0, 0))],
        out_specs=pl.BlockSpec((seq, hw), lambda b, h: (b, h)),
        out_shape=jax.ShapeDtypeStruct((batch * seq, ATTN_WIDTH), jnp.bfloat16),
        scratch_shapes=[pltpu.VMEM((HEADS_PER_STEP, 2, Q_BLK, K_WIN), jnp.float32),
                        pltpu.VMEM((HEADS_PER_STEP, 2, Q_BLK, K_WIN), jnp.bfloat16),
                        pltpu.VMEM((HEADS_PER_STEP, 2, Q_BLK, 1), jnp.float32)],
        compiler_params=_params(("parallel", "arbitrary")),
        name="band_attn",
    )(proj, proj, proj, proj, bias)


def _merge_kernel(ya_ref, yp_ref, wa_ref, wp_ref, ga_ref, gp_ref, gb_ref, o_ref):
    a = jnp.dot(ya_ref[...], wa_ref[...], preferred_element_type=jnp.float32)
    p = jnp.dot(yp_ref[...], wp_ref[...], preferred_element_type=jnp.float32)
    ga = jax.nn.sigmoid(ga_ref[...].astype(jnp.float32) + gb_ref[0:1, :])
    gp = jax.nn.sigmoid(gp_ref[...].astype(jnp.float32) + gb_ref[1:2, :])
    o_ref[...] = (ga * a + gp * p).astype(o_ref.dtype)


def _merge(y_attn, y_pool, wa, wp, proj_hi, gate_bias, bm=1024, bn=1024):
    m = y_attn.shape[0]
    ga_col = (GA_OFF - HALF_COLS) // bn
    gp_col = (GP_OFF - HALF_COLS) // bn
    return pl.pallas_call(
        _merge_kernel,
        grid=(m // bm, D_MODEL // bn),
        in_specs=[pl.BlockSpec((bm, ATTN_WIDTH), lambda i, j: (i, 0)),
                  pl.BlockSpec((bm, POOL_WIDTH), lambda i, j: (i, 0)),
                  pl.BlockSpec((ATTN_WIDTH, bn), lambda i, j: (0, j)),
                  pl.BlockSpec((POOL_WIDTH, bn), lambda i, j: (0, j)),
                  pl.BlockSpec((bm, bn), lambda i, j: (i, ga_col + j)),
                  pl.BlockSpec((bm, bn), lambda i, j: (i, gp_col + j)),
                  pl.BlockSpec((2, bn), lambda i, j: (0, j))],
        out_specs=pl.BlockSpec((bm, bn), lambda i, j: (i, j)),
        out_shape=jax.ShapeDtypeStruct((m, D_MODEL), jnp.bfloat16),
        compiler_params=_params(("parallel", "arbitrary")),
        name="gated_merge",
    )(y_attn, y_pool, wa, wp, proj_hi, proj_hi, gate_bias)


def _out_kernel(m_ref, w_ref, x_ref, g_ref, o_hbm, r_scr, ssq_scr, scale_scr, sem, *, n_row, n_col):
    i = pl.program_id(0)
    j = pl.program_id(1)
    _, bm, bn = r_scr.shape
    d = n_col * bn

    def out_copy(row_blk, col_blk):
        return pltpu.make_async_copy(
            r_scr.at[col_blk],
            o_hbm.at[pl.ds(row_blk * bm, bm), pl.ds(col_blk * bn, bn)],
            sem.at[col_blk])

    @pl.when(i > 0)
    def _():
        out_copy(i - 1, j).wait()

    acc = jnp.dot(m_ref[...], w_ref[...], preferred_element_type=jnp.float32) + x_ref[...]
    r_scr[j] = acc
    sq = acc * acc
    part = sq[:, 0:LANES]
    for c in range(1, bn // LANES):
        part = part + sq[:, c * LANES:(c + 1) * LANES]

    @pl.when(j == 0)
    def _():
        ssq_scr[...] = part

    @pl.when(j > 0)
    def _():
        ssq_scr[...] += part

    @pl.when(j == n_col - 1)
    def _():
        ms = jnp.sum(ssq_scr[...], axis=-1, keepdims=True) * (1.0 / d)
        scale_scr[...] = jnp.broadcast_to(lax.rsqrt(ms + EPS), scale_scr.shape)
        for jj in range(n_col):
            def scale_rows(r, carry):
                rows = pl.ds(pl.multiple_of(r * NORM_ROWS, NORM_ROWS), NORM_ROWS)
                sc = scale_scr[rows, :]
                for c in range(bn // LANES):
                    cols = slice(c * LANES, (c + 1) * LANES)
                    gain = g_ref[:, jj * bn + c * LANES:jj * bn + (c + 1) * LANES]
                    r_scr[jj, rows, cols] = r_scr[jj, rows, cols] * sc * gain
                return carry

            lax.fori_loop(0, bm // NORM_ROWS, scale_rows, 0)
            out_copy(i, jj).start()

        @pl.when(i == n_row - 1)
        def _():
            for jj in range(n_col):
                out_copy(i, jj).wait()


def _out_proj_norm(m_arr, w_out, x2, gain, bm=1024, bn=512):
    m, d = x2.shape
    n_row, n_col = m // bm, d // bn
    return pl.pallas_call(
        functools.partial(_out_kernel, n_row=n_row, n_col=n_col),
        grid=(n_row, n_col),
        in_specs=[pl.BlockSpec((bm, d), lambda i, j: (i, 0)),
                  pl.BlockSpec((d, bn), lambda i, j: (0, j)),
                  pl.BlockSpec((bm, bn), lambda i, j: (i, j)),
                  pl.BlockSpec((1, d), lambda i, j: (0, 0))],
        out_specs=pl.BlockSpec(memory_space=pl.ANY),
        out_shape=jax.ShapeDtypeStruct((m, d), jnp.float32),
        scratch_shapes=[pltpu.VMEM((n_col, bm, bn), jnp.float32),
                        pltpu.VMEM((bm, LANES), jnp.float32),
                        pltpu.VMEM((bm, LANES), jnp.float32),
                        pltpu.SemaphoreType.DMA((n_col,))],
        compiler_params=_params(("arbitrary", "arbitrary")),
        name="out_proj_norm",
    )(m_arr, w_out, x2, gain.reshape(1, d))


def kernel(x, norm_gain, w_in, rel_bias, pool_w, pool_scale, w_out_attn, w_out_pool,
           gate_bias, w_out, final_gain):
    batch, seq, d = x.shape
    x2 = x.reshape(batch * seq, d)
    bf = jnp.bfloat16

    col_scale = jnp.where(jnp.arange(IN_COLS) < ATTN_WIDTH, SCORE_SCALE, 1.0).astype(jnp.float32)
    w_hi = w_in[:, HALF_COLS:].astype(bf)
    proj_hi, h, (w_lo,) = _norm_proj(
        x2, norm_gain, w_hi, col_scale[HALF_COLS:], [CastJob(w_in, 32, HALF_COLS, 0)])
    proj_lo, (wa, wp, wo) = _proj(
        h, w_lo, col_scale[:HALF_COLS],
        [CastJob(w_out_attn, 64, D_MODEL, 0), CastJob(w_out_pool, 64, D_MODEL, 0),
         CastJob(w_out, 64, D_MODEL, 0)])
    y_pool = _pooling(proj_lo, proj_hi, pool_w, pool_scale, batch, seq)
    y_attn = _attention(proj_lo, _band_bias(rel_bias), batch, seq)
    m = _merge(y_attn, y_pool, wa, wp, proj_hi, gate_bias)
    out = _out_proj_norm(m, wo, x2, final_gain)
    return out.reshape(batch, seq, d)
```
